```python
import jax
import jax.numpy as jnp
from jax import lax
import numpy as np

D_MODEL = 1024
BATCH = 8
SEQ = 2048
DEPTH = 1
DEC_BATCH = 32
DEC_SEQ = 64
PAST_LEN = 2048

CHUNK = 64
N_PAST_CHUNKS = 8
ATTN_REACH = N_PAST_CHUNKS * CHUNK
BAND = (N_PAST_CHUNKS + 1) * CHUNK
N_HEADS_A = 8
HEAD_DIM_A = 64
D_ATTN = N_HEADS_A * HEAD_DIM_A
REL_CLIP = 256
N_REL = 2 * REL_CLIP + 1
ATTN_SCALE = HEAD_DIM_A ** -0.5
N_HEADS_B = 4
DK_B = 128
DV_B = 128
D_RNN = N_HEADS_B * DK_B
IN_WIDTHS = (D_ATTN, D_ATTN, D_ATTN, D_RNN, D_RNN, D_RNN, D_RNN, D_MODEL, D_MODEL)
D_IN = sum(IN_WIDTHS)
IN_SPLITS = tuple(sum(IN_WIDTHS[:i + 1]) for i in range(len(IN_WIDTHS) - 1))
PEER_HEADS = 8
N_KEYS = 128
N_EXPERTS = N_KEYS * N_KEYS
PEER_HALF = 128
PEER_TOPK = 16
PEER_BLOCK = 128
EPS = 1e-6

kernel_name = 'streaming_hybrid_bandattn_hgrn2_peer'


def rmsnorm(x, gain):
    xf = x.astype(jnp.float32)
    y = xf * lax.rsqrt(jnp.mean(xf * xf, axis=-1, keepdims=True) + EPS)
    return (y * gain.astype(jnp.float32)).astype(x.dtype)


def mixer_proj(x, norm_g, w_in, lb):
    b, t, _ = x.shape
    z = rmsnorm(x, norm_g) @ w_in
    q_a, k_a, v_a, f_b, i_b, q_b, g_b, gate_a, gate_b = jnp.split(z, IN_SPLITS, axis=-1)

    def heads_a(a):
        return a.reshape(b, t, N_HEADS_A, HEAD_DIM_A)

    def heads_b(a):
        return a.reshape(b, t, N_HEADS_B, -1).transpose(0, 2, 1, 3).astype(jnp.float32)

    lbh = lb.reshape(N_HEADS_B, 1, DK_B)
    fg = lbh + (1.0 - lbh) * jax.nn.sigmoid(heads_b(f_b))
    logf = jnp.log(fg)
    k_b = 1.0 - fg
    qb = jax.nn.silu(heads_b(q_b))
    vb = heads_b(i_b)
    return heads_a(q_a), heads_a(k_a), heads_a(v_a), qb, k_b, vb, logf, g_b, gate_a, gate_b


def band_attention_prompt(q, k, v, rel_bias):
    b, s, h, d = q.shape
    nc = s // CHUNK
    qc = q.reshape(b, nc, CHUNK, h, d)
    pad = ((0, 0), (ATTN_REACH, 0), (0, 0), (0, 0))
    kc = jnp.pad(k, pad).reshape(b, nc + N_PAST_CHUNKS, CHUNK, h, d)
    vc = jnp.pad(v, pad).reshape(b, nc + N_PAST_CHUNKS, CHUNK, h, d)
    kb = jnp.concatenate([kc[:, i:i + nc] for i in range(N_PAST_CHUNKS + 1)], axis=2)
    vb = jnp.concatenate([vc[:, i:i + nc] for i in range(N_PAST_CHUNKS + 1)], axis=2)
    t_idx = jnp.arange(CHUNK)[:, None]
    j_idx = jnp.arange(BAND)[None, :]
    dist = ATTN_REACH + t_idx - j_idx
    bias = rel_bias[:, jnp.clip(dist, -REL_CLIP, REL_CLIP) + REL_CLIP].astype(jnp.float32)
    key_pos = (jnp.arange(nc)[:, None] - N_PAST_CHUNKS) * CHUNK + j_idx
    scores = jnp.einsum('bnthd,bnjhd->bnhtj', qc, kb).astype(jnp.float32) * ATTN_SCALE + bias
    scores = jnp.where((key_pos >= 0)[None, :, None, None, :], scores, -jnp.inf)
    p = jax.nn.softmax(scores, axis=-1).astype(v.dtype)
    return jnp.einsum('bnhtj,bnjhd->bnthd', p, vb).reshape(b, s, h, d)


def band_attention_step(q, k_new, v_new, k_cache, v_cache, rel_bias):
    past = k_cache.shape[1]
    t = q.shape[1]
    keys = jnp.concatenate([k_cache.astype(k_new.dtype), k_new], axis=1)
    vals = jnp.concatenate([v_cache.astype(v_new.dtype), v_new], axis=1)
    dist = past + jnp.arange(t)[:, None] - jnp.arange(past + t)[None, :]
    bias = rel_bias[:, jnp.clip(dist, -REL_CLIP, REL_CLIP) + REL_CLIP].astype(jnp.float32)
    scores = jnp.einsum('bthd,bjhd->bhtj', q, keys).astype(jnp.float32) * ATTN_SCALE + bias
    p = jax.nn.softmax(scores, axis=-1).astype(vals.dtype)
    return jnp.einsum('bhtj,bjhd->bthd', p, vals)


def hgrn2_chunk(s0, q, k, v, logf):
    t = q.shape[2]
    cum = jnp.cumsum(logf, axis=2)
    causal = jnp.arange(t)[:, None] >= jnp.arange(t)[None, :]
    decay = jnp.exp(jnp.where(causal[..., None], cum[:, :, :, None, :] - cum[:, :, None, :, :], -jnp.inf))
    attn = jnp.einsum('bhtk,bhsk,bhtsk->bhts', q, k, decay)
    o = jnp.einsum('bhtk,bhkv->bhtv', q * jnp.exp(cum), s0) + jnp.einsum('bhts,bhsv->bhtv', attn, v)
    last = cum[:, :, -1:, :]
    s_new = jnp.exp(last[:, :, 0, :])[..., None] * s0 + jnp.einsum('bhsk,bhsv->bhkv', k * jnp.exp(last - cum), v)
    return s_new, o


def hgrn2_prompt(q, k, v, logf):
    b, h, s, _ = q.shape
    nc = s // CHUNK

    def to_chunks(a):
        return a.reshape(b, h, nc, CHUNK, a.shape[-1]).transpose(2, 0, 1, 3, 4)

    s0 = jnp.zeros((b, h, DK_B, DV_B), jnp.float32)

    def step(state, inp):
        return hgrn2_chunk(state, *inp)

    s_fin, o = lax.scan(step, s0, (to_chunks(q), to_chunks(k), to_chunks(v), to_chunks(logf)))
    return s_fin, o.transpose(1, 2, 0, 3, 4).reshape(b, h, s, DV_B)


def peer(x, w_query, sub_keys, u_tab, v_tab):
    shp = x.shape
    xt = x.reshape(-1, D_MODEL)
    n = xt.shape[0]
    q = (xt @ w_query).reshape(n, PEER_HEADS, 2, PEER_HALF)
    s = jnp.einsum('nhpd,phkd->nhpk', q, sub_keys).astype(jnp.float32)
    sv, si = lax.top_k(s, PEER_TOPK)
    comb = (sv[:, :, 0, :, None] + sv[:, :, 1, None, :]).reshape(n, PEER_HEADS, PEER_TOPK * PEER_TOPK)
    cv, ci = lax.top_k(comb, PEER_TOPK)
    i1 = jnp.take_along_axis(si[:, :, 0], ci // PEER_TOPK, axis=-1)
    i2 = jnp.take_along_axis(si[:, :, 1], ci % PEER_TOPK, axis=-1)
    idx = i1 * N_KEYS + i2
    gate = jax.nn.softmax(cv, axis=-1)
    pad = (-n) % PEER_BLOCK
    nb = (n + pad) // PEER_BLOCK
    xt_p = jnp.pad(xt, ((0, pad), (0, 0))).reshape(nb, PEER_BLOCK, D_MODEL)
    idx_p = jnp.pad(idx, ((0, pad), (0, 0), (0, 0))).reshape(nb, PEER_BLOCK, PEER_HEADS, PEER_TOPK)
    gate_p = jnp.pad(gate, ((0, pad), (0, 0), (0, 0))).reshape(nb, PEER_BLOCK, PEER_HEADS, PEER_TOPK)

    def block(args):
        xb, ib, gb = args
        hid = jax.nn.gelu(jnp.einsum('nd,nhkd->nhk', xb, u_tab[ib]).astype(jnp.float32), approximate=False)
        return jnp.einsum('nhk,nhkd->nd', (gb * hid).astype(v_tab.dtype), v_tab[ib])

    out = lax.map(block, (xt_p, idx_p, gate_p))
    return out.reshape(-1, D_MODEL)[:n].reshape(shp).astype(x.dtype)


def finish_layer(x, o_a, o_b, g_b, gate_a, gate_b, hgrn_g, w_a, w_b, w_out,
                 norm_ffn, w_query, sub_keys, u_tab, v_tab):
    b, t, _ = x.shape
    oa = o_a.reshape(b, t, D_ATTN)
    ob = o_b.transpose(0, 2, 1, 3)
    ob = ob * lax.rsqrt(jnp.mean(ob * ob, axis=-1, keepdims=True) + EPS) * hgrn_g.astype(jnp.float32)
    ob = ob * jax.nn.silu(g_b.reshape(b, t, N_HEADS_B, DV_B).astype(jnp.float32))
    ob = ob.reshape(b, t, D_RNN).astype(x.dtype)
    merged = jax.nn.sigmoid(gate_a) * (oa @ w_a) + jax.nn.sigmoid(gate_b) * (ob @ w_b)
    h = x + merged @ w_out
    return h + peer(rmsnorm(h, norm_ffn), w_query, sub_keys, u_tab, v_tab)


def setup_inputs(seed: int = 0) -> dict:
    key = jax.random.key(seed)
    ks = jax.random.split(key, 20)

    def nrm(k, shape, scale):
        return jax.random.normal(k, shape, jnp.float32) * scale

    a_len = min(ATTN_REACH, PAST_LEN)
    return {
        'x_prompt': nrm(ks[0], (BATCH, SEQ, D_MODEL), 1.0),
        'x_sample': nrm(ks[1], (DEC_BATCH, DEC_SEQ, D_MODEL), 1.0),
        'cache_attn_k': nrm(ks[2], (DEPTH, DEC_BATCH, a_len, N_HEADS_A, HEAD_DIM_A), 1.0),
        'cache_attn_v': nrm(ks[3], (DEPTH, DEC_BATCH, a_len, N_HEADS_A, HEAD_DIM_A), 1.0),
        'state_hgrn': nrm(ks[4], (DEPTH, DEC_BATCH, N_HEADS_B, DK_B, DV_B), 0.5),
        'norm_mix': 1.0 + nrm(ks[5], (DEPTH, D_MODEL), 0.05),
        'w_in': nrm(ks[6], (DEPTH, D_MODEL, D_IN), D_MODEL ** -0.5),
        'rel_bias': nrm(ks[7], (DEPTH, N_HEADS_A, N_REL), 0.2),
        'lb_logits': nrm(ks[8], (DEPTH + 1, D_RNN), 0.1),
        'hgrn_norm': 1.0 + nrm(ks[9], (DEPTH, N_HEADS_B, DV_B), 0.05),
        'w_branch_a': nrm(ks[10], (DEPTH, D_ATTN, D_MODEL), D_ATTN ** -0.5),
        'w_branch_b': nrm(ks[11], (DEPTH, D_RNN, D_MODEL), D_RNN ** -0.5),
        'w_out': nrm(ks[12], (DEPTH, D_MODEL, D_MODEL), D_MODEL ** -0.5),
        'norm_ffn': 1.0 + nrm(ks[13], (DEPTH, D_MODEL), 0.05),
        'peer_query': nrm(ks[14], (DEPTH, D_MODEL, PEER_HEADS * 2 * PEER_HALF), D_MODEL ** -0.5),
        'peer_subkeys': nrm(ks[15], (DEPTH, 2, PEER_HEADS, N_KEYS, PEER_HALF), PEER_HALF ** -0.5),
        'peer_u': nrm(ks[16], (DEPTH, N_EXPERTS, D_MODEL), D_MODEL ** -0.5),
        'peer_v': nrm(ks[17], (DEPTH, N_EXPERTS, D_MODEL), PEER_HEADS ** -0.5),
        'norm_final': 1.0 + nrm(ks[18], (D_MODEL,), 0.05),
    }


def reference(x_prompt, x_sample, cache_attn_k, cache_attn_v, state_hgrn, norm_mix, w_in, rel_bias,
              lb_logits, hgrn_norm, w_branch_a, w_branch_b, w_out, norm_ffn, peer_query, peer_subkeys,
              peer_u, peer_v, norm_final):
    lb_all = jnp.cumsum(jax.nn.softmax(lb_logits.astype(jnp.float32), axis=0), axis=0)
    xp = x_prompt
    xs = x_sample
    keep = min(ATTN_REACH, x_prompt.shape[1])
    kp_rows, vp_rows, sp_states = [], [], []
    ks_rows, vs_rows, ss_states = [], [], []
    for l in range(DEPTH):
        lb = lb_all[l]
        qa, ka, va, qb, kb, vb, logf, gb, ga_gate, gb_gate = mixer_proj(xp, norm_mix[l], w_in[l], lb)
        oa = band_attention_prompt(qa, ka, va, rel_bias[l])
        s_fin, ob = hgrn2_prompt(qb, kb, vb, logf)
        kp_rows.append(ka[:, ka.shape[1] - keep:])
        vp_rows.append(va[:, va.shape[1] - keep:])
        sp_states.append(s_fin.astype(x_prompt.dtype))
        xp = finish_layer(xp, oa, ob, gb, ga_gate, gb_gate, hgrn_norm[l], w_branch_a[l], w_branch_b[l],
                          w_out[l], norm_ffn[l], peer_query[l], peer_subkeys[l], peer_u[l], peer_v[l])
        qa, ka, va, qb, kb, vb, logf, gb, ga_gate, gb_gate = mixer_proj(xs, norm_mix[l], w_in[l], lb)
        oa = band_attention_step(qa, ka, va, cache_attn_k[l], cache_attn_v[l], rel_bias[l])
        s_new, ob = hgrn2_chunk(state_hgrn[l].astype(jnp.float32), qb, kb, vb, logf)
        ks_rows.append(ka)
        vs_rows.append(va)
        ss_states.append(s_new.astype(x_sample.dtype))
        xs = finish_layer(xs, oa, ob, gb, ga_gate, gb_gate, hgrn_norm[l], w_branch_a[l], w_branch_b[l],
                          w_out[l], norm_ffn[l], peer_query[l], peer_subkeys[l], peer_u[l], peer_v[l])
    y_prompt = rmsnorm(xp, norm_final)
    y_sample = rmsnorm(xs, norm_final)
    new_k_prompt = jnp.stack(kp_rows, axis=0)
    new_v_prompt = jnp.stack(vp_rows, axis=0)
    new_state_prompt = jnp.stack(sp_states, axis=0)
    new_k_sample = jnp.stack(ks_rows, axis=0)
    new_v_sample = jnp.stack(vs_rows, axis=0)
    new_state_sample = jnp.stack(ss_states, axis=0)
    return (y_prompt, y_sample, new_k_prompt, new_v_prompt, new_state_prompt, new_k_sample, new_v_sample, new_state_sample)
```

```python
import functools
import math

import jax
import jax.numpy as jnp
from jax import lax
from jax.experimental import pallas as pl
from jax.experimental.pallas import tpu as pltpu

F32 = jnp.float32
BF16 = jnp.bfloat16

D_MODEL = 1024
CHUNK = 64
N_PAST_CHUNKS = 8
ATTN_REACH = N_PAST_CHUNKS * CHUNK
N_HEADS_A = 8
HEAD_DIM_A = 64
D_ATTN = N_HEADS_A * HEAD_DIM_A
REL_CLIP = 256
ATTN_SCALE = HEAD_DIM_A ** -0.5
N_HEADS_B = 4
DK_B = 128
DV_B = 128
D_RNN = N_HEADS_B * DK_B
PEER_HEADS = 8
N_KEYS = 128
N_EXPERTS = N_KEYS * N_KEYS
PEER_HALF = 128
PEER_TOPK = 16
EPS = 1e-6

LANES = 128
Q_BLOCK = 4 * CHUNK
SUB = 16
MASKED = -1e30
VMEM_LIMIT = 56 * 1024 * 1024


def _params(*sem):
    return pltpu.CompilerParams(dimension_semantics=sem, vmem_limit_bytes=VMEM_LIMIT)


def _const_spec(shape):
    nd = len(shape)
    return pl.BlockSpec(shape, lambda *_: (0,) * nd, pipeline_mode=pl.Buffered(1))


def _row_block(n, cap):
    t = cap
    while n % t:
        t //= 2
    assert t >= LANES, (n, cap)
    return t


def _rmsnorm(x, g):
    return x * lax.rsqrt(jnp.mean(x * x, axis=-1, keepdims=True) + EPS) * g


W_A = 3 * D_ATTN
W_B = 4 * D_RNN
W_G = 2 * D_MODEL


def _inproj_body(x_ref, g_ref, w_ref, za_ref, zb_ref, zg_ref):
    xn = _rmsnorm(x_ref[...], g_ref[...]).astype(BF16)
    za_ref[...] = jnp.dot(xn, w_ref[:, 0:W_A], preferred_element_type=F32)
    zb_ref[...] = jnp.dot(xn, w_ref[:, W_A:W_A + W_B], preferred_element_type=F32)
    zg_ref[...] = jnp.dot(xn, w_ref[:, W_A + W_B:W_A + W_B + W_G], preferred_element_type=F32)


def _inproj(x2, g, w):
    n = x2.shape[0]
    tm = _row_block(n, 256)
    return pl.pallas_call(
        _inproj_body,
        grid=(n // tm,),
        in_specs=[pl.BlockSpec((tm, D_MODEL), lambda i: (i, 0)),
                  _const_spec((1, D_MODEL)),
                  _const_spec((D_MODEL, W_A + W_B + W_G))],
        out_specs=[pl.BlockSpec((tm, W_A), lambda i: (i, 0)),
                   pl.BlockSpec((tm, W_B), lambda i: (i, 0)),
                   pl.BlockSpec((tm, W_G), lambda i: (i, 0))],
        out_shape=[jax.ShapeDtypeStruct((n, W_A), F32),
                   jax.ShapeDtypeStruct((n, W_B), F32),
                   jax.ShapeDtypeStruct((n, W_G), F32)],
        compiler_params=_params("parallel"),
        name="inproj",
    )(x2, g, w)


def _attend_pair(q2, parts, bias_of):
    lane = lax.broadcasted_iota(jnp.int32, (1, LANES), 1)
    out = None
    for sub in range(2):
        m = (lane >= HEAD_DIM_A) if sub else (lane < HEAD_DIM_A)
        qm = jnp.where(m, q2, 0.0).astype(BF16)
        scores = []
        for pi, (k2, _) in enumerate(parts):
            s = lax.dot_general(qm, k2, (((1,), (1,)), ((), ())), preferred_element_type=F32)
            scores.append(s * ATTN_SCALE + bias_of(sub, pi))
        mx = functools.reduce(jnp.maximum, [jnp.max(s, axis=-1, keepdims=True) for s in scores])
        l = None
        o = None
        for s, (_, v2) in zip(scores, parts):
            p = jnp.exp(s - mx)
            ls = jnp.sum(p, axis=-1, keepdims=True)
            os_ = jnp.dot(p.astype(BF16), v2, preferred_element_type=F32)
            l = ls if l is None else l + ls
            o = os_ if o is None else o + os_
        o = o / l
        out = o if out is None else jnp.where(m, o, out)
    return out


def _attn_prompt_body(q_ref, k_ref, v_ref, bias_ref, o_ref):
    qi = pl.program_id(1)
    win = Q_BLOCK + ATTN_REACH

    def run(kstart, nk, boff):
        for hp in range(N_HEADS_A // 2):
            ls = slice(hp * LANES, (hp + 1) * LANES)
            k2 = k_ref[0, pl.ds(kstart, nk), ls].astype(BF16)
            v2 = v_ref[0, pl.ds(kstart, nk), ls].astype(BF16)
            o_ref[0, :, ls] = _attend_pair(
                q_ref[0, :, ls], [(k2, v2)],
                lambda sub, pi: bias_ref[2 * hp + sub, :, boff:boff + nk])

    n_short = ATTN_REACH // Q_BLOCK
    for j in range(n_short):
        nk = (j + 1) * Q_BLOCK
        pl.when(qi == j)(functools.partial(run, 0, nk, win - nk))

    @pl.when(qi >= n_short)
    def _():
        run(pl.multiple_of(qi * Q_BLOCK - ATTN_REACH, Q_BLOCK), win, 0)


def _attn_prompt(za3, bias):
    b, s, _ = za3.shape
    assert s % Q_BLOCK == 0 and ATTN_REACH % Q_BLOCK == 0
    return pl.pallas_call(
        _attn_prompt_body,
        grid=(b, s // Q_BLOCK),
        in_specs=[pl.BlockSpec((1, Q_BLOCK, D_ATTN), lambda i, j: (i, j, 0)),
                  pl.BlockSpec((1, s, D_ATTN), lambda i, j: (i, 0, 1)),
                  pl.BlockSpec((1, s, D_ATTN), lambda i, j: (i, 0, 2)),
                  _const_spec(bias.shape)],
        out_specs=pl.BlockSpec((1, Q_BLOCK, D_ATTN), lambda i, j: (i, j, 0)),
        out_shape=jax.ShapeDtypeStruct((b, s, D_ATTN), F32),
        compiler_params=_params("parallel", "arbitrary"),
        name="attn_prompt",
    )(za3, za3, za3, bias)


def _attn_step_body(q_ref, kn_ref, vn_ref, kc_ref, vc_ref, bc_ref, bn_ref, o_ref):
    for hp in range(N_HEADS_A // 2):
        ls = slice(hp * LANES, (hp + 1) * LANES)
        parts = [(kc_ref[0, :, ls].astype(BF16), vc_ref[0, :, ls].astype(BF16)),
                 (kn_ref[0, :, ls].astype(BF16), vn_ref[0, :, ls].astype(BF16))]
        o_ref[0, :, ls] = _attend_pair(
            q_ref[0, :, ls], parts,
            lambda sub, pi: (bc_ref, bn_ref)[pi][2 * hp + sub])


def _attn_step(za3, kc3, vc3, bias_c, bias_n):
    b, t, _ = za3.shape
    past = kc3.shape[1]
    return pl.pallas_call(
        _attn_step_body,
        grid=(b,),
        in_specs=[pl.BlockSpec((1, t, D_ATTN), lambda i: (i, 0, 0)),
                  pl.BlockSpec((1, t, D_ATTN), lambda i: (i, 0, 1)),
                  pl.BlockSpec((1, t, D_ATTN), lambda i: (i, 0, 2)),
                  pl.BlockSpec((1, past, D_ATTN), lambda i: (i, 0, 0)),
                  pl.BlockSpec((1, past, D_ATTN), lambda i: (i, 0, 0)),
                  _const_spec(bias_c.shape),
                  _const_spec(bias_n.shape)],
        out_specs=pl.BlockSpec((1, t, D_ATTN), lambda i: (i, 0, 0)),
        out_shape=jax.ShapeDtypeStruct((b, t, D_ATTN), F32),
        compiler_params=_params("parallel"),
        name="attn_step",
    )(za3, za3, za3, kc3, vc3, bias_c, bias_n)


def _rel_bias_tables(rel_bias):
    def table(tq, nk, past):
        t = jnp.arange(tq)[:, None]
        j = jnp.arange(nk)[None, :]
        dist = past + t - j
        return dist, rel_bias[:, jnp.clip(dist, -REL_CLIP, REL_CLIP) + REL_CLIP].astype(F32)

    win = Q_BLOCK + ATTN_REACH
    t = jnp.arange(Q_BLOCK)[:, None] // CHUNK
    j = jnp.arange(win)[None, :] // CHUNK
    _, full = table(Q_BLOCK, win, ATTN_REACH)
    band = (j >= t) & (j <= t + N_PAST_CHUNKS)
    return jnp.where(band[None], full, MASKED)


def _step_bias_tables(rel_bias, past, t_new):
    t = jnp.arange(t_new)[:, None]
    j = jnp.arange(past + t_new)[None, :]
    dist = past + t - j
    full = rel_bias[:, jnp.clip(dist, -REL_CLIP, REL_CLIP) + REL_CLIP].astype(F32)
    return full[:, :, :past], full[:, :, past:]


def _cumsum_rows(x):
    row = lax.broadcasted_iota(jnp.int32, x.shape, 0)
    sh = 1
    while sh < x.shape[0]:
        x = x + jnp.where(row >= sh, pltpu.roll(x, sh, axis=0), 0.0)
        sh *= 2
    return x


def _hgrn_chunk(f, iv, qr, g, lbh, hgh, st_t):
    fg = lbh + (1.0 - lbh) * jax.nn.sigmoid(f)
    logf = jnp.log(fg)
    kk = 1.0 - fg
    qq = qr * jax.nn.sigmoid(qr)
    cum = _cumsum_rows(logf)
    vb = iv.astype(BF16)
    nsub = CHUNK // SUB

    qe = (qq * jnp.exp(cum)).astype(BF16)
    o = lax.dot_general(qe, st_t.astype(BF16), (((1,), (1,)), ((), ())), preferred_element_type=F32)

    off = [jnp.zeros((SUB, DV_B), F32)]
    for bi in range(1, nsub):
        lo = bi * SUB
        ref = cum[lo - 1:lo, :]
        a = (qq[lo:lo + SUB] * jnp.exp(cum[lo:lo + SUB] - ref)).astype(BF16)
        bm = (kk[0:lo] * jnp.exp(ref - cum[0:lo])).astype(BF16)
        att = lax.dot_general(a, bm, (((1,), (1,)), ((), ())), preferred_element_type=F32)
        off.append(jnp.dot(att.astype(BF16), vb[0:lo], preferred_element_type=F32))
    o = o + jnp.concatenate(off, axis=0)

    rmod = lax.broadcasted_iota(jnp.int32, (CHUNK, DK_B), 0) % SUB
    for d in range(SUB):
        ks, cs, vs = (kk, cum, iv) if d == 0 else (
            pltpu.roll(kk, d, axis=0), pltpu.roll(cum, d, axis=0), pltpu.roll(iv, d, axis=0))
        p = jnp.where(rmod >= d, qq * ks * jnp.exp(cum - cs), 0.0)
        o = o + jnp.sum(p, axis=-1, keepdims=True) * vs

    last = cum[CHUNK - 1:CHUNK, :]
    kd = (kk * jnp.exp(last - cum)).astype(BF16)
    st_new = st_t * jnp.exp(last) + jnp.dot(iv.T.astype(BF16), kd, preferred_element_type=F32)

    ob = o * lax.rsqrt(jnp.mean(o * o, axis=-1, keepdims=True) + EPS) * hgh
    ob = ob * (g * jax.nn.sigmoid(g))
    return ob, st_new


def _hgrn_body(*refs, has_s0):
    if has_s0:
        f_ref, i_ref, q_ref, g_ref, lb_ref, hg_ref, s0_ref, o_ref, sn_ref, st_ref = refs
    else:
        f_ref, i_ref, q_ref, g_ref, lb_ref, hg_ref, o_ref, sn_ref, st_ref = refs
    c = pl.program_id(1)

    @pl.when(c == 0)
    def _():
        for h in range(N_HEADS_B):
            st_ref[h] = s0_ref[0, h].T if has_s0 else jnp.zeros((DV_B, DK_B), F32)

    for h in range(N_HEADS_B):
        ls = slice(h * LANES, (h + 1) * LANES)
        ob, st_new = _hgrn_chunk(f_ref[0, :, ls], i_ref[0, :, ls], q_ref[0, :, ls], g_ref[0, :, ls],
                                 lb_ref[:, ls], hg_ref[:, ls], st_ref[h])
        o_ref[0, :, ls] = ob
        st_ref[h] = st_new

    @pl.when(c == pl.num_programs(1) - 1)
    def _():
        for h in range(N_HEADS_B):
            sn_ref[0, h] = st_ref[h].T


def _hgrn(zb3, lb, hg, s0):
    b, s, _ = zb3.shape
    nc = s // CHUNK
    col = lambda k: pl.BlockSpec((1, CHUNK, D_RNN), lambda i, j: (i, j, k))
    in_specs = [col(0), col(1), col(2), col(3), _const_spec((1, D_RNN)), _const_spec((1, D_RNN))]
    args = [zb3, zb3, zb3, zb3, lb, hg]
    if s0 is not None:
        in_specs.append(pl.BlockSpec((1, N_HEADS_B, DK_B, DV_B), lambda i, j: (i, 0, 0, 0)))
        args.append(s0)
    return pl.pallas_call(
        functools.partial(_hgrn_body, has_s0=s0 is not None),
        grid=(b, nc),
        in_specs=in_specs,
        out_specs=[pl.BlockSpec((1, CHUNK, D_RNN), lambda i, j: (i, j, 0)),
                   pl.BlockSpec((1, N_HEADS_B, DK_B, DV_B), lambda i, j: (i, 0, 0, 0))],
        out_shape=[jax.ShapeDtypeStruct((b, s, D_RNN), F32),
                   jax.ShapeDtypeStruct((b, N_HEADS_B, DK_B, DV_B), F32)],
        scratch_shapes=[pltpu.VMEM((N_HEADS_B, DV_B, DK_B), F32)],
        compiler_params=_params("parallel", "arbitrary"),
        name="hgrn",
    )(*args)


def _finish_body(x_ref, oa_ref, ob_ref, zg_ref, wa_ref, wb_ref, wo_ref, h_ref):
    ma = jnp.dot(oa_ref[...].astype(BF16), wa_ref[...], preferred_element_type=F32)
    mb = jnp.dot(ob_ref[...].astype(BF16), wb_ref[...], preferred_element_type=F32)
    merged = (jax.nn.sigmoid(zg_ref[:, 0:D_MODEL]) * ma
              + jax.nn.sigmoid(zg_ref[:, D_MODEL:2 * D_MODEL]) * mb)
    h_ref[...] = x_ref[...] + jnp.dot(merged.astype(BF16), wo_ref[...], preferred_element_type=F32)


def _finish(x2, oa2, ob2, zg, wa, wb, wo):
    n = x2.shape[0]
    tm = _row_block(n, 512)
    row = lambda w: pl.BlockSpec((tm, w), lambda i: (i, 0))
    return pl.pallas_call(
        _finish_body,
        grid=(n // tm,),
        in_specs=[row(D_MODEL), row(D_ATTN), row(D_RNN), row(W_G),
                  _const_spec(wa.shape), _const_spec(wb.shape), _const_spec(wo.shape)],
        out_specs=row(D_MODEL),
        out_shape=jax.ShapeDtypeStruct((n, D_MODEL), F32),
        compiler_params=_params("parallel"),
        name="finish",
    )(x2, oa2, ob2, zg, wa, wb, wo)


N_TOP = PEER_TOPK + 1
E_BLOCK = 1024
ROWS_PER_STEP = E_BLOCK // N_KEYS


def _top_rows(s, n):
    rows = []
    for r in range(n):
        mx = jnp.max(s, axis=0, keepdims=True)
        rows.append(mx)
        if r + 1 < n:
            s = jnp.where(s == mx, -jnp.inf, s)
    return rows


def _peer_select(hd, qt_ref, sk_ref, s2_ref, th_ref, e1_ref, e2_ref):
    t = qt_ref.shape[1]
    sc = []
    for p in range(2):
        qhp = qt_ref[pl.ds(pl.multiple_of((hd * 2 + p) * PEER_HALF, PEER_HALF), PEER_HALF), :]
        sc.append(jnp.dot(sk_ref[p, hd], qhp.astype(BF16), preferred_element_type=F32))
    s1, s2 = sc
    a = _top_rows(s1, N_TOP)
    b = _top_rows(s2, N_TOP)
    pad = 8 * ((N_TOP + 7) // 8)
    b_all = jnp.concatenate(b + [jnp.full((pad - N_TOP, t), -jnp.inf, F32)], axis=0)
    cand = jnp.concatenate([a[0] + b_all] + [a[p] + b_all[0:8] for p in range(1, N_TOP)], axis=0)
    c = _top_rows(cand, N_TOP)
    tau = 0.5 * (c[PEER_TOPK - 1] + c[PEER_TOPK])
    z = functools.reduce(lambda x, y: x + y, [jnp.exp(cr - c[0]) for cr in c[:PEER_TOPK]])
    th_ref[hd] = tau - s1
    e1_ref[hd] = jnp.exp(s1 - a[0]) * (1.0 / z)
    s2_ref[hd] = s2
    e2_ref[hd] = jnp.exp(s2 - b[0])


def _peer_body(h_ref, gf_ref, wq_ref, sk_ref, u_ref, vt_ref, gl_ref, y_ref,
               xt_ref, qt_ref, s2_ref, th_ref, e1_ref, e2_ref, hid_ref, wg_ref, acc_ref):
    e = pl.program_id(1)

    @pl.when(e == 0)
    def _():
        hn = _rmsnorm(h_ref[...], gf_ref[...])
        xt_ref[...] = hn.T.astype(BF16)
        qt_ref[...] = jnp.dot(wq_ref[...], xt_ref[...], preferred_element_type=F32)

        def head(hd, carry):
            _peer_select(hd, qt_ref, sk_ref, s2_ref, th_ref, e1_ref, e2_ref)
            return carry

        lax.fori_loop(0, PEER_HEADS, head, 0)
        acc_ref[...] = jnp.zeros_like(acc_ref)

    hid_ref[...] = jnp.dot(u_ref[...], xt_ref[...], preferred_element_type=F32)

    def key_row(r, carry):
        i = e * ROWS_PER_STEP + r
        w = None
        for hd in range(PEER_HEADS):
            sel = s2_ref[hd] >= th_ref[hd, pl.ds(i, 1), :]
            wh = jnp.where(sel, e2_ref[hd] * e1_ref[hd, pl.ds(i, 1), :], 0.0)
            w = wh if w is None else w + wh
        rows = pl.ds(pl.multiple_of(r * N_KEYS, N_KEYS), N_KEYS)
        hid = hid_ref[rows, :]
        act = 0.5 * hid * (1.0 + lax.erf(hid * math.sqrt(0.5)))
        wg_ref[rows, :] = (w * act).astype(BF16)
        return carry

    lax.fori_loop(0, ROWS_PER_STEP, key_row, 0)
    acc_ref[...] += jnp.dot(vt_ref[...], wg_ref[...], preferred_element_type=F32)

    @pl.when(e == pl.num_programs(1) - 1)
    def _():
        y_ref[...] = _rmsnorm(h_ref[...] + acc_ref[...].T, gl_ref[...])


def _peer(h2, g_ffn, wq_t, sk, u, v_t, g_last):
    n = h2.shape[0]
    t = _row_block(n, 512)
    return pl.pallas_call(
        _peer_body,
        grid=(n // t, N_EXPERTS // E_BLOCK),
        in_specs=[pl.BlockSpec((t, D_MODEL), lambda i, e: (i, 0)),
                  _const_spec((1, D_MODEL)),
                  _const_spec(wq_t.shape),
                  _const_spec(sk.shape),
                  pl.BlockSpec((E_BLOCK, D_MODEL), lambda i, e: (e, 0)),
                  pl.BlockSpec((D_MODEL, E_BLOCK), lambda i, e: (0, e)),
                  _const_spec((1, D_MODEL))],
        out_specs=pl.BlockSpec((t, D_MODEL), lambda i, e: (i, 0)),
        out_shape=jax.ShapeDtypeStruct((n, D_MODEL), F32),
        scratch_shapes=[pltpu.VMEM((D_MODEL, t), BF16),
                        pltpu.VMEM((PEER_HEADS * 2 * PEER_HALF, t), F32),
                        pltpu.VMEM((PEER_HEADS, N_KEYS, t), F32),
                        pltpu.VMEM((PEER_HEADS, N_KEYS, t), F32),
                        pltpu.VMEM((PEER_HEADS, N_KEYS, t), F32),
                        pltpu.VMEM((PEER_HEADS, N_KEYS, t), F32),
                        pltpu.VMEM((E_BLOCK, t), F32),
                        pltpu.VMEM((E_BLOCK, t), BF16),
                        pltpu.VMEM((D_MODEL, t), F32)],
        compiler_params=_params("parallel", "arbitrary"),
        name="peer",
    )(h2, g_ffn, wq_t, sk, u, v_t, g_last)


def _layer(x3, s0, kc3, vc3, w, bias_tabs):
    b, s, _ = x3.shape
    x2 = x3.reshape(b * s, D_MODEL)
    za, zb, zg = _inproj(x2, w["norm_mix"], w["w_in"])
    za3 = za.reshape(b, s, W_A)
    if kc3 is None:
        oa = _attn_prompt(za3, bias_tabs)
    else:
        oa = _attn_step(za3, kc3, vc3, *bias_tabs)
    ob, s_new = _hgrn(zb.reshape(b, s, W_B), w["lb"], w["hgrn_norm"], s0)
    h2 = _finish(x2, oa.reshape(b * s, D_ATTN), ob.reshape(b * s, D_RNN), zg,
                 w["w_a"], w["w_b"], w["w_out"])
    k_rows = za3[:, :, D_ATTN:2 * D_ATTN]
    v_rows = za3[:, :, 2 * D_ATTN:3 * D_ATTN]
    return h2, k_rows, v_rows, s_new


def kernel(x_prompt, x_sample, cache_attn_k, cache_attn_v, state_hgrn, norm_mix, w_in, rel_bias,
           lb_logits, hgrn_norm, w_branch_a, w_branch_b, w_out, norm_ffn, peer_query, peer_subkeys,
           peer_u, peer_v, norm_final):
    depth = w_in.shape[0]
    bp, sp, _ = x_prompt.shape
    bs, ss, _ = x_sample.shape
    past = cache_attn_k.shape[2]
    keep = min(ATTN_REACH, sp)
    lb_all = jnp.cumsum(jax.nn.softmax(lb_logits.astype(F32), axis=0), axis=0)
    g_last = norm_final.reshape(1, D_MODEL)

    xp, xs = x_prompt, x_sample
    outs = [[] for _ in range(6)]
    for l in range(depth):
        w = dict(
            norm_mix=norm_mix[l].reshape(1, D_MODEL),
            w_in=w_in[l].astype(BF16),
            lb=lb_all[l].reshape(1, D_RNN),
            hgrn_norm=hgrn_norm[l].reshape(1, D_RNN),
            w_a=w_branch_a[l].astype(BF16),
            w_b=w_branch_b[l].astype(BF16),
            w_out=w_out[l].astype(BF16),
        )
        g_ffn = norm_ffn[l].reshape(1, D_MODEL)
        wq_t = peer_query[l].T.astype(BF16)
        sk = peer_subkeys[l].astype(BF16)
        u = peer_u[l].astype(BF16)
        v_t = peer_v[l].T.astype(BF16)
        last = l == depth - 1

        hp, kp, vp, stp = _layer(xp, None, None, None, w, _rel_bias_tables(rel_bias[l]))
        hs, ks, vs, sts = _layer(
            xs, state_hgrn[l],
            cache_attn_k[l].reshape(bs, past, D_ATTN), cache_attn_v[l].reshape(bs, past, D_ATTN),
            w, _step_bias_tables(rel_bias[l], past, ss))
        assert last, "only the final layer fuses the closing rmsnorm"
        xp = _peer(hp, g_ffn, wq_t, sk, u, v_t, g_last).reshape(bp, sp, D_MODEL)
        xs = _peer(hs, g_ffn, wq_t, sk, u, v_t, g_last).reshape(bs, ss, D_MODEL)
        outs[0].append(kp[:, sp - keep:].reshape(bp, keep, N_HEADS_A, HEAD_DIM_A))
        outs[1].append(vp[:, sp - keep:].reshape(bp, keep, N_HEADS_A, HEAD_DIM_A))
        outs[2].append(stp)
        outs[3].append(ks.reshape(bs, ss, N_HEADS_A, HEAD_DIM_A))
        outs[4].append(vs.reshape(bs, ss, N_HEADS_A, HEAD_DIM_A))
        outs[5].append(sts)
    stacked = [jnp.stack(o, axis=0) for o in outs]
    return (xp, xs, *stacked)
```

```python
import functools
import math

import jax
import jax.numpy as jnp
from jax import lax
from jax.experimental import pallas as pl
from jax.experimental.pallas import tpu as pltpu

F32 = jnp.float32
BF16 = jnp.bfloat16

D_MODEL = 1024
CHUNK = 64
N_PAST_CHUNKS = 8
ATTN_REACH = N_PAST_CHUNKS * CHUNK
N_HEADS_A = 8
HEAD_DIM_A = 64
D_ATTN = N_HEADS_A * HEAD_DIM_A
REL_CLIP = 256
ATTN_SCALE = HEAD_DIM_A ** -0.5
N_HEADS_B = 4
DK_B = 128
DV_B = 128
D_RNN = N_HEADS_B * DK_B
PEER_HEADS = 8
N_KEYS = 128
N_EXPERTS = N_KEYS * N_KEYS
PEER_HALF = 128
PEER_TOPK = 16
EPS = 1e-6

LANES = 128
Q_BLOCK = 4 * CHUNK
SUB = 16
MASKED = -1e30
VMEM_LIMIT = 56 * 1024 * 1024


def _params(*sem):
    return pltpu.CompilerParams(dimension_semantics=sem, vmem_limit_bytes=VMEM_LIMIT)


def _const_spec(shape):
    nd = len(shape)
    return pl.BlockSpec(shape, lambda *_: (0,) * nd, pipeline_mode=pl.Buffered(1))


def _row_block(n, cap):
    t = cap
    while n % t:
        t //= 2
    assert t >= LANES, (n, cap)
    return t


def _rmsnorm(x, g):
    return x * lax.rsqrt(jnp.mean(x * x, axis=-1, keepdims=True) + EPS) * g


W_A = 3 * D_ATTN
W_B = 4 * D_RNN
W_G = 2 * D_MODEL


def _inproj_body(x_ref, g_ref, w_ref, za_ref, zb_ref, zg_ref):
    xn = _rmsnorm(x_ref[...], g_ref[...]).astype(BF16)
    za_ref[...] = jnp.dot(xn, w_ref[:, 0:W_A], preferred_element_type=F32)
    zb_ref[...] = jnp.dot(xn, w_ref[:, W_A:W_A + W_B], preferred_element_type=F32)
    zg_ref[...] = jnp.dot(xn, w_ref[:, W_A + W_B:W_A + W_B + W_G], preferred_element_type=F32)


def _inproj(x2, g, w):
    n = x2.shape[0]
    tm = _row_block(n, 256)
    return pl.pallas_call(
        _inproj_body,
        grid=(n // tm,),
        in_specs=[pl.BlockSpec((tm, D_MODEL), lambda i: (i, 0)),
                  _const_spec((1, D_MODEL)),
                  _const_spec((D_MODEL, W_A + W_B + W_G))],
        out_specs=[pl.BlockSpec((tm, W_A), lambda i: (i, 0)),
                   pl.BlockSpec((tm, W_B), lambda i: (i, 0)),
                   pl.BlockSpec((tm, W_G), lambda i: (i, 0))],
        out_shape=[jax.ShapeDtypeStruct((n, W_A), F32),
                   jax.ShapeDtypeStruct((n, W_B), F32),
                   jax.ShapeDtypeStruct((n, W_G), F32)],
        compiler_params=_params("parallel"),
        name="inproj",
    )(x2, g, w)


def _attend_pair(q2, parts, bias_of):
    lane = lax.broadcasted_iota(jnp.int32, (1, LANES), 1)
    out = None
    for sub in range(2):
        m = (lane >= HEAD_DIM_A) if sub else (lane < HEAD_DIM_A)
        qm = jnp.where(m, q2, 0.0).astype(BF16)
        scores = []
        for pi, (k2, _) in enumerate(parts):
            s = lax.dot_general(qm, k2, (((1,), (1,)), ((), ())), preferred_element_type=F32)
            scores.append(s * ATTN_SCALE + bias_of(sub, pi))
        mx = functools.reduce(jnp.maximum, [jnp.max(s, axis=-1, keepdims=True) for s in scores])
        l = None
        o = None
        for s, (_, v2) in zip(scores, parts):
            p = jnp.exp(s - mx)
            ls = jnp.sum(p, axis=-1, keepdims=True)
            os_ = jnp.dot(p.astype(BF16), v2, preferred_element_type=F32)
            l = ls if l is None else l + ls
            o = os_ if o is None else o + os_
        o = o / l
        out = o if out is None else jnp.where(m, o, out)
    return out


def _attn_prompt_body(q_ref, k_ref, v_ref, bias_ref, o_ref):
    qi = pl.program_id(1)
    win = Q_BLOCK + ATTN_REACH

    def run(kstart, nk, boff):
        for hp in range(N_HEADS_A // 2):
            ls = slice(hp * LANES, (hp + 1) * LANES)
            k2 = k_ref[0, pl.ds(kstart, nk), ls].astype(BF16)
            v2 = v_ref[0, pl.ds(kstart, nk), ls].astype(BF16)
            o_ref[0, :, ls] = _attend_pair(
                q_ref[0, :, ls], [(k2, v2)],
                lambda sub, pi: bias_ref[2 * hp + sub, :, boff:boff + nk])

    n_short = ATTN_REACH // Q_BLOCK
    for j in range(n_short):
        nk = (j + 1) * Q_BLOCK
        pl.when(qi == j)(functools.partial(run, 0, nk, win - nk))

    @pl.when(qi >= n_short)
    def _():
        run(pl.multiple_of(qi * Q_BLOCK - ATTN_REACH, Q_BLOCK), win, 0)


def _attn_prompt(za3, bias):
    b, s, _ = za3.shape
    assert s % Q_BLOCK == 0 and ATTN_REACH % Q_BLOCK == 0
    return pl.pallas_call(
        _attn_prompt_body,
        grid=(b, s // Q_BLOCK),
        in_specs=[pl.BlockSpec((1, Q_BLOCK, D_ATTN), lambda i, j: (i, j, 0)),
                  pl.BlockSpec((1, s, D_ATTN), lambda i, j: (i, 0, 1)),
                  pl.BlockSpec((1, s, D_ATTN), lambda i, j: (i, 0, 2)),
                  _const_spec(bias.shape)],
        out_specs=pl.BlockSpec((1, Q_BLOCK, D_ATTN), lambda i, j: (i, j, 0)),
        out_shape=jax.ShapeDtypeStruct((b, s, D_ATTN), F32),
        compiler_params=_params("parallel", "arbitrary"),
        name="attn_prompt",
    )(za3, za3, za3, bias)


def _attn_step_body(q_ref, kn_ref, vn_ref, kc_ref, vc_ref, bc_ref, bn_ref, o_ref):
    for hp in range(N_HEADS_A // 2):
        ls = slice(hp * LANES, (hp + 1) * LANES)
        parts = [(kc_ref[0, :, ls].astype(BF16), vc_ref[0, :, ls].astype(BF16)),
                 (kn_ref[0, :, ls].astype(BF16), vn_ref[0, :, ls].astype(BF16))]
        o_ref[0, :, ls] = _attend_pair(
            q_ref[0, :, ls], parts,
            lambda sub, pi: (bc_ref, bn_ref)[pi][2 * hp + sub])


def _attn_step(za3, kc3, vc3, bias_c, bias_n):
    b, t, _ = za3.shape
    past = kc3.shape[1]
    return pl.pallas_call(
        _attn_step_body,
        grid=(b,),
        in_specs=[pl.BlockSpec((1, t, D_ATTN), lambda i: (i, 0, 0)),
                  pl.BlockSpec((1, t, D_ATTN), lambda i: (i, 0, 1)),
                  pl.BlockSpec((1, t, D_ATTN), lambda i: (i, 0, 2)),
                  pl.BlockSpec((1, past, D_ATTN), lambda i: (i, 0, 0)),
                  pl.BlockSpec((1, past, D_ATTN), lambda i: (i, 0, 0)),
                  _const_spec(bias_c.shape),
                  _const_spec(bias_n.shape)],
        out_specs=pl.BlockSpec((1, t, D_ATTN), lambda i: (i, 0, 0)),
        out_shape=jax.ShapeDtypeStruct((b, t, D_ATTN), F32),
        compiler_params=_params("parallel"),
        name="attn_step",
    )(za3, za3, za3, kc3, vc3, bias_c, bias_n)


def _toeplitz_bias(rel_bias, tq, nk, past):
    span = tq + nk - 1
    lo = past - (nk - 1) + REL_CLIP
    padl = max(0, -lo)
    padr = max(0, lo + span - 1 - 2 * REL_CLIP)
    ext = jnp.pad(rel_bias.astype(F32), ((0, 0), (padl, padr)), mode="edge")
    g = jnp.pad(ext[:, lo + padl:lo + padl + span], ((0, 0), (0, 1)))
    q = span + 1
    hankel = jnp.tile(g, (1, tq + 1))[:, :tq * (q + 1)].reshape(-1, tq, q + 1)[:, :, :nk]
    return hankel[:, :, ::-1]


def _rel_bias_tables(rel_bias):
    win = Q_BLOCK + ATTN_REACH
    t = jnp.arange(Q_BLOCK)[:, None] // CHUNK
    j = jnp.arange(win)[None, :] // CHUNK
    band = (j >= t) & (j <= t + N_PAST_CHUNKS)
    return jnp.where(band[None], _toeplitz_bias(rel_bias, Q_BLOCK, win, ATTN_REACH), MASKED)


def _step_bias_tables(rel_bias, past, t_new):
    full = _toeplitz_bias(rel_bias, t_new, past + t_new, past)
    return full[:, :, :past], full[:, :, past:]


def _cumsum_rows(x):
    row = lax.broadcasted_iota(jnp.int32, x.shape, 0)
    sh = 1
    while sh < x.shape[0]:
        x = x + jnp.where(row >= sh, pltpu.roll(x, sh, axis=0), 0.0)
        sh *= 2
    return x


def _hgrn_chunk(f, iv, qr, g, lbh, hgh, st_t):
    fg = lbh + (1.0 - lbh) * jax.nn.sigmoid(f)
    logf = jnp.log(fg)
    kk = 1.0 - fg
    qq = qr * jax.nn.sigmoid(qr)
    cum = _cumsum_rows(logf)
    vb = iv.astype(BF16)
    nsub = CHUNK // SUB

    qe = (qq * jnp.exp(cum)).astype(BF16)
    o = lax.dot_general(qe, st_t.astype(BF16), (((1,), (1,)), ((), ())), preferred_element_type=F32)

    off = [jnp.zeros((SUB, DV_B), F32)]
    for bi in range(1, nsub):
        lo = bi * SUB
        ref = cum[lo - 1:lo, :]
        a = (qq[lo:lo + SUB] * jnp.exp(cum[lo:lo + SUB] - ref)).astype(BF16)
        bm = (kk[0:lo] * jnp.exp(ref - cum[0:lo])).astype(BF16)
        att = lax.dot_general(a, bm, (((1,), (1,)), ((), ())), preferred_element_type=F32)
        off.append(jnp.dot(att.astype(BF16), vb[0:lo], preferred_element_type=F32))
    o = o + jnp.concatenate(off, axis=0)

    rmod = lax.broadcasted_iota(jnp.int32, (CHUNK, DK_B), 0) % SUB
    for d in range(SUB):
        ks, cs, vs = (kk, cum, iv) if d == 0 else (
            pltpu.roll(kk, d, axis=0), pltpu.roll(cum, d, axis=0), pltpu.roll(iv, d, axis=0))
        p = jnp.where(rmod >= d, qq * ks * jnp.exp(cum - cs), 0.0)
        o = o + jnp.sum(p, axis=-1, keepdims=True) * vs

    last = cum[CHUNK - 1:CHUNK, :]
    kd = (kk * jnp.exp(last - cum)).astype(BF16)
    st_new = st_t * jnp.exp(last) + jnp.dot(iv.T.astype(BF16), kd, preferred_element_type=F32)

    ob = o * lax.rsqrt(jnp.mean(o * o, axis=-1, keepdims=True) + EPS) * hgh
    ob = ob * (g * jax.nn.sigmoid(g))
    return ob, st_new


def _hgrn_body(*refs, has_s0):
    if has_s0:
        f_ref, i_ref, q_ref, g_ref, lb_ref, hg_ref, s0_ref, o_ref, sn_ref, st_ref = refs
    else:
        f_ref, i_ref, q_ref, g_ref, lb_ref, hg_ref, o_ref, sn_ref, st_ref = refs
    c = pl.program_id(1)

    @pl.when(c == 0)
    def _():
        for h in range(N_HEADS_B):
            st_ref[h] = s0_ref[0, h].T if has_s0 else jnp.zeros((DV_B, DK_B), F32)

    for h in range(N_HEADS_B):
        ls = slice(h * LANES, (h + 1) * LANES)
        ob, st_new = _hgrn_chunk(f_ref[0, :, ls], i_ref[0, :, ls], q_ref[0, :, ls], g_ref[0, :, ls],
                                 lb_ref[:, ls], hg_ref[:, ls], st_ref[h])
        o_ref[0, :, ls] = ob
        st_ref[h] = st_new

    @pl.when(c == pl.num_programs(1) - 1)
    def _():
        for h in range(N_HEADS_B):
            sn_ref[0, h] = st_ref[h].T


def _hgrn(zb3, lb, hg, s0):
    b, s, _ = zb3.shape
    nc = s // CHUNK
    col = lambda k: pl.BlockSpec((1, CHUNK, D_RNN), lambda i, j: (i, j, k))
    in_specs = [col(0), col(1), col(2), col(3), _const_spec((1, D_RNN)), _const_spec((1, D_RNN))]
    args = [zb3, zb3, zb3, zb3, lb, hg]
    if s0 is not None:
        in_specs.append(pl.BlockSpec((1, N_HEADS_B, DK_B, DV_B), lambda i, j: (i, 0, 0, 0)))
        args.append(s0)
    return pl.pallas_call(
        functools.partial(_hgrn_body, has_s0=s0 is not None),
        grid=(b, nc),
        in_specs=in_specs,
        out_specs=[pl.BlockSpec((1, CHUNK, D_RNN), lambda i, j: (i, j, 0)),
                   pl.BlockSpec((1, N_HEADS_B, DK_B, DV_B), lambda i, j: (i, 0, 0, 0))],
        out_shape=[jax.ShapeDtypeStruct((b, s, D_RNN), F32),
                   jax.ShapeDtypeStruct((b, N_HEADS_B, DK_B, DV_B), F32)],
        scratch_shapes=[pltpu.VMEM((N_HEADS_B, DV_B, DK_B), F32)],
        compiler_params=_params("parallel", "arbitrary"),
        name="hgrn",
    )(*args)


def _finish_body(x_ref, oa_ref, ob_ref, zg_ref, wa_ref, wb_ref, wo_ref, h_ref):
    ma = jnp.dot(oa_ref[...].astype(BF16), wa_ref[...], preferred_element_type=F32)
    mb = jnp.dot(ob_ref[...].astype(BF16), wb_ref[...], preferred_element_type=F32)
    merged = (jax.nn.sigmoid(zg_ref[:, 0:D_MODEL]) * ma
              + jax.nn.sigmoid(zg_ref[:, D_MODEL:2 * D_MODEL]) * mb)
    h_ref[...] = x_ref[...] + jnp.dot(merged.astype(BF16), wo_ref[...], preferred_element_type=F32)


def _finish(x2, oa2, ob2, zg, wa, wb, wo):
    n = x2.shape[0]
    tm = _row_block(n, 512)
    row = lambda w: pl.BlockSpec((tm, w), lambda i: (i, 0))
    return pl.pallas_call(
        _finish_body,
        grid=(n // tm,),
        in_specs=[row(D_MODEL), row(D_ATTN), row(D_RNN), row(W_G),
                  _const_spec(wa.shape), _const_spec(wb.shape), _const_spec(wo.shape)],
        out_specs=row(D_MODEL),
        out_shape=jax.ShapeDtypeStruct((n, D_MODEL), F32),
        compiler_params=_params("parallel"),
        name="finish",
    )(x2, oa2, ob2, zg, wa, wb, wo)


N_TOP = PEER_TOPK + 1
E_BLOCK = D_MODEL
ROWS_PER_STEP = E_BLOCK // N_KEYS
N_SLOTS = 2
ROW_GROUP = 8


NO_RANK = 64


def _top_rows(s, n, want_rank=False):
    rows = []
    rank = jnp.full(s.shape, float(NO_RANK), F32) if want_rank else None
    for r in range(n):
        mx = jnp.max(s, axis=0, keepdims=True)
        rows.append(mx)
        hit = s == mx
        if want_rank:
            rank = jnp.where(hit, float(r), rank)
        if r + 1 < n:
            s = jnp.where(hit, -jnp.inf, s)
    return rows, rank


def _peer_select(hd, qt_ref, sk_ref, rk_ref, cn_ref, e1_ref, e2_ref):
    t = qt_ref.shape[1]
    sc = []
    for p in range(2):
        qhp = qt_ref[pl.ds(pl.multiple_of((hd * 2 + p) * PEER_HALF, PEER_HALF), PEER_HALF), :]
        sc.append(jnp.dot(sk_ref[p, hd], qhp.astype(BF16), preferred_element_type=F32))
    s1, s2 = sc
    a, _ = _top_rows(s1, N_TOP)
    b, rank2 = _top_rows(s2, N_TOP, want_rank=True)
    pad = 8 * ((N_TOP + 7) // 8)
    b_all = jnp.concatenate(b + [jnp.full((pad - N_TOP, t), -jnp.inf, F32)], axis=0)
    cand = jnp.concatenate([a[0] + b_all] + [a[p] + b_all[0:8] for p in range(1, N_TOP)], axis=0)
    c, _ = _top_rows(cand, N_TOP)
    tau = 0.5 * (c[PEER_TOPK - 1] + c[PEER_TOPK])
    z = functools.reduce(lambda x, y: x + y, [jnp.exp(cr - c[0]) for cr in c[:PEER_TOPK]])
    th = tau - s1
    cn_ref[hd] = functools.reduce(lambda x, y: x + y, [jnp.where(bq >= th, 1.0, 0.0) for bq in b])
    e1_ref[hd] = jnp.exp(s1 - a[0]) * (1.0 / z)
    rk_ref[hd] = rank2.astype(BF16)
    e2_ref[hd] = jnp.exp(s2 - b[0]).astype(BF16)


def _peer_body(h_ref, gf_ref, wq_ref, sk_ref, u_hbm, vt_hbm, gl_ref, y_ref,
               xt_ref, qt_ref, rk_ref, cn_ref, e1_ref, e2_ref, hid0, hid1, wg0, wg1, acc_ref,
               u_buf, vt_buf, u_sem, vt_sem):
    ne = u_hbm.shape[0]
    hid_refs, wg_refs = (hid0, hid1), (wg0, wg1)

    def u_copy(e, slot):
        return pltpu.make_async_copy(u_hbm.at[e], u_buf.at[slot], u_sem.at[slot])

    def vt_copy(e, slot):
        return pltpu.make_async_copy(vt_hbm.at[e], vt_buf.at[slot], vt_sem.at[slot])

    def start_if_in_range(copy_of, e, slot):
        if isinstance(e, int):
            if e < ne:
                copy_of(e, slot).start()
        else:
            pl.when(e < ne)(lambda: copy_of(e, slot).start())

    def stage_a(rows, slot, par):
        hid_refs[par][rows, :] = jnp.dot(u_buf[slot, rows, :], xt_ref[...], preferred_element_type=F32)

    def stage_b(i, rows, par):
        w = None
        for hd in range(PEER_HEADS):
            cnt = cn_ref[hd, pl.ds(i, 1), :].astype(BF16)
            e1 = e1_ref[hd, pl.ds(i, 1), :].astype(BF16)
            wh = jnp.where(rk_ref[hd] < cnt, e2_ref[hd], jnp.zeros((), BF16)) * e1
            w = wh if w is None else w + wh
        hid = hid_refs[par][rows, :]
        act = 0.5 * hid * (1.0 + lax.erf(hid * math.sqrt(0.5)))
        wg_refs[par][rows, :] = w * act.astype(BF16)

    def stage_c(rows, slot, par, first):
        d = jnp.dot(vt_buf[slot, rows, :], wg_refs[par][...], preferred_element_type=F32)
        if first:
            acc_ref[rows, :] = d
        else:
            acc_ref[rows, :] += d

    def iteration(e, par):
        static = isinstance(e, int)
        do_a = (not static) or e < ne
        do_b = (not static) or 1 <= e <= ne
        do_c = (not static) or e >= 2
        sa, sc = e % N_SLOTS, (e - 2) % N_SLOTS
        if do_a:
            u_copy(e, sa).wait()
        if do_c:
            vt_copy(e - 2, sc).wait()

        def row_group(g, carry):
            nrow = ROW_GROUP * N_KEYS
            rows = pl.ds(pl.multiple_of(g * nrow, nrow), nrow)
            if do_a:
                stage_a(rows, sa, par)
            if do_b:
                for k in range(ROW_GROUP):
                    r = g * ROW_GROUP + k
                    stage_b((e - 1) * ROWS_PER_STEP + r,
                            pl.ds(pl.multiple_of(r * N_KEYS, N_KEYS), N_KEYS), 1 - par)
            if do_c:
                stage_c(rows, sc, par, static and e == 2)
            return carry

        lax.fori_loop(0, ROWS_PER_STEP // ROW_GROUP, row_group, 0)
        if do_a:
            start_if_in_range(u_copy, e + N_SLOTS, sa)
        if do_c:
            start_if_in_range(vt_copy, e - 2 + N_SLOTS, sc)

    for s in range(N_SLOTS):
        u_copy(s, s).start()
        vt_copy(s, s).start()

    hn = _rmsnorm(h_ref[...], gf_ref[...])
    xt_ref[...] = hn.T.astype(BF16)
    qt_ref[...] = jnp.dot(wq_ref[...], xt_ref[...], preferred_element_type=F32)

    def head(hd, carry):
        _peer_select(hd, qt_ref, sk_ref, rk_ref, cn_ref, e1_ref, e2_ref)
        return carry

    lax.fori_loop(0, PEER_HEADS, head, 0)

    def steady(k, carry):
        iteration(2 * k, 0)
        iteration(2 * k + 1, 1)
        return carry

    assert ne % 2 == 0
    for e in range(4):
        iteration(e, e % 2)
    lax.fori_loop(2, ne // 2, steady, 0)
    for e in range(ne, ne + 2):
        iteration(e, e % 2)

    y_ref[...] = _rmsnorm(h_ref[...] + acc_ref[...].T, gl_ref[...])


def _peer(h2, g_ffn, wq_t, sk, u, v_t, g_last):
    n = h2.shape[0]
    t = _row_block(n, 512)
    assert u.shape[0] >= N_SLOTS + 2
    return pl.pallas_call(
        _peer_body,
        grid=(n // t,),
        in_specs=[pl.BlockSpec((t, D_MODEL), lambda i: (i, 0)),
                  _const_spec((1, D_MODEL)),
                  _const_spec(wq_t.shape),
                  _const_spec(sk.shape),
                  pl.BlockSpec(memory_space=pl.ANY),
                  pl.BlockSpec(memory_space=pl.ANY),
                  _const_spec((1, D_MODEL))],
        out_specs=pl.BlockSpec((t, D_MODEL), lambda i: (i, 0)),
        out_shape=jax.ShapeDtypeStruct((n, D_MODEL), F32),
        scratch_shapes=[pltpu.VMEM((D_MODEL, t), BF16),
                        pltpu.VMEM((PEER_HEADS * 2 * PEER_HALF, t), F32),
                        pltpu.VMEM((PEER_HEADS, N_KEYS, t), BF16),
                        pltpu.VMEM((PEER_HEADS, N_KEYS, t), F32),
                        pltpu.VMEM((PEER_HEADS, N_KEYS, t), F32),
                        pltpu.VMEM((PEER_HEADS, N_KEYS, t), BF16),
                        pltpu.VMEM((E_BLOCK, t), F32),
                        pltpu.VMEM((E_BLOCK, t), F32),
                        pltpu.VMEM((E_BLOCK, t), BF16),
                        pltpu.VMEM((E_BLOCK, t), BF16),
                        pltpu.VMEM((D_MODEL, t), F32),
                        pltpu.VMEM((N_SLOTS, E_BLOCK, D_MODEL), BF16),
                        pltpu.VMEM((N_SLOTS, D_MODEL, E_BLOCK), BF16),
                        pltpu.SemaphoreType.DMA((N_SLOTS,)),
                        pltpu.SemaphoreType.DMA((N_SLOTS,))],
        compiler_params=_params("parallel"),
        name="peer",
    )(h2, g_ffn, wq_t, sk, u, v_t, g_last)


def _layer(x3, s0, kc3, vc3, w, bias_tabs):
    b, s, _ = x3.shape
    x2 = x3.reshape(b * s, D_MODEL)
    za, zb, zg = _inproj(x2, w["norm_mix"], w["w_in"])
    za3 = za.reshape(b, s, W_A)
    if kc3 is None:
        oa = _attn_prompt(za3, bias_tabs)
    else:
        oa = _attn_step(za3, kc3, vc3, *bias_tabs)
    ob, s_new = _hgrn(zb.reshape(b, s, W_B), w["lb"], w["hgrn_norm"], s0)
    h2 = _finish(x2, oa.reshape(b * s, D_ATTN), ob.reshape(b * s, D_RNN), zg,
                 w["w_a"], w["w_b"], w["w_out"])
    k_rows = za3[:, :, D_ATTN:2 * D_ATTN]
    v_rows = za3[:, :, 2 * D_ATTN:3 * D_ATTN]
    return h2, k_rows, v_rows, s_new


def kernel(x_prompt, x_sample, cache_attn_k, cache_attn_v, state_hgrn, norm_mix, w_in, rel_bias,
           lb_logits, hgrn_norm, w_branch_a, w_branch_b, w_out, norm_ffn, peer_query, peer_subkeys,
           peer_u, peer_v, norm_final):
    depth = w_in.shape[0]
    bp, sp, _ = x_prompt.shape
    bs, ss, _ = x_sample.shape
    past = cache_attn_k.shape[2]
    keep = min(ATTN_REACH, sp)
    lb_all = jnp.cumsum(jax.nn.softmax(lb_logits.astype(F32), axis=0), axis=0)
    g_last = norm_final.reshape(1, D_MODEL)

    xp, xs = x_prompt, x_sample
    outs = [[] for _ in range(6)]
    for l in range(depth):
        w = dict(
            norm_mix=norm_mix[l].reshape(1, D_MODEL),
            w_in=w_in[l].astype(BF16),
            lb=lb_all[l].reshape(1, D_RNN),
            hgrn_norm=hgrn_norm[l].reshape(1, D_RNN),
            w_a=w_branch_a[l].astype(BF16),
            w_b=w_branch_b[l].astype(BF16),
            w_out=w_out[l].astype(BF16),
        )
        g_ffn = norm_ffn[l].reshape(1, D_MODEL)
        wq_t = peer_query[l].T.astype(BF16)
        sk = peer_subkeys[l].astype(BF16)
        ne = N_EXPERTS // E_BLOCK
        u = peer_u[l].astype(BF16).reshape(ne, E_BLOCK, D_MODEL)
        v_t = peer_v[l].astype(BF16).reshape(ne, E_BLOCK, D_MODEL).transpose(0, 2, 1)
        last = l == depth - 1

        hp, kp, vp, stp = _layer(xp, None, None, None, w, _rel_bias_tables(rel_bias[l]))
        hs, ks, vs, sts = _layer(
            xs, state_hgrn[l],
            cache_attn_k[l].reshape(bs, past, D_ATTN), cache_attn_v[l].reshape(bs, past, D_ATTN),
            w, _step_bias_tables(rel_bias[l], past, ss))
        assert last, "only the final layer fuses the closing rmsnorm"
        xp = _peer(hp, g_ffn, wq_t, sk, u, v_t, g_last).reshape(bp, sp, D_MODEL)
        xs = _peer(hs, g_ffn, wq_t, sk, u, v_t, g_last).reshape(bs, ss, D_MODEL)
        outs[0].append(kp[:, sp - keep:].reshape(bp, keep, N_HEADS_A, HEAD_DIM_A))
        outs[1].append(vp[:, sp - keep:].reshape(bp, keep, N_HEADS_A, HEAD_DIM_A))
        outs[2].append(stp)
        outs[3].append(ks.reshape(bs, ss, N_HEADS_A, HEAD_DIM_A))
        outs[4].append(vs.reshape(bs, ss, N_HEADS_A, HEAD_DIM_A))
        outs[5].append(sts)
    stacked = [jnp.stack(o, axis=0) for o in outs]
    return (xp, xs, *stacked)
```

```python
import functools
import math

import jax
import jax.numpy as jnp
from jax import lax
from jax.experimental import pallas as pl
from jax.experimental.pallas import tpu as pltpu

F32 = jnp.float32
BF16 = jnp.bfloat16

D_MODEL = 1024
CHUNK = 64
N_PAST_CHUNKS = 8
ATTN_REACH = N_PAST_CHUNKS * CHUNK
N_HEADS_A = 8
HEAD_DIM_A = 64
D_ATTN = N_HEADS_A * HEAD_DIM_A
REL_CLIP = 256
ATTN_SCALE = HEAD_DIM_A ** -0.5
N_HEADS_B = 4
DK_B = 128
DV_B = 128
D_RNN = N_HEADS_B * DK_B
PEER_HEADS = 8
N_KEYS = 128
N_EXPERTS = N_KEYS * N_KEYS
PEER_HALF = 128
PEER_TOPK = 16
EPS = 1e-6

LANES = 128
Q_BLOCK = 4 * CHUNK
SUB = 16
MASKED = -1e30
VMEM_LIMIT = 56 * 1024 * 1024


def _params(*sem):
    return pltpu.CompilerParams(dimension_semantics=sem, vmem_limit_bytes=VMEM_LIMIT)


def _const_spec(shape):
    nd = len(shape)
    return pl.BlockSpec(shape, lambda *_: (0,) * nd, pipeline_mode=pl.Buffered(1))


def _row_block(n, cap):
    t = cap
    while n % t:
        t //= 2
    assert t >= LANES, (n, cap)
    return t


def _rmsnorm(x, g):
    return x * lax.rsqrt(jnp.mean(x * x, axis=-1, keepdims=True) + EPS) * g


W_A = 3 * D_ATTN
W_B = 4 * D_RNN
W_G = 2 * D_MODEL


def _inproj_body(x_ref, g_ref, w_ref, za_ref, zb_ref, zg_ref):
    xn = _rmsnorm(x_ref[...], g_ref[...]).astype(BF16)
    za_ref[...] = jnp.dot(xn, w_ref[:, 0:W_A], preferred_element_type=F32)
    zb_ref[...] = jnp.dot(xn, w_ref[:, W_A:W_A + W_B], preferred_element_type=F32)
    zg_ref[...] = jnp.dot(xn, w_ref[:, W_A + W_B:W_A + W_B + W_G], preferred_element_type=F32)


def _inproj(x2, g, w):
    n = x2.shape[0]
    tm = _row_block(n, 256)
    return pl.pallas_call(
        _inproj_body,
        grid=(n // tm,),
        in_specs=[pl.BlockSpec((tm, D_MODEL), lambda i: (i, 0)),
                  _const_spec((1, D_MODEL)),
                  _const_spec((D_MODEL, W_A + W_B + W_G))],
        out_specs=[pl.BlockSpec((tm, W_A), lambda i: (i, 0)),
                   pl.BlockSpec((tm, W_B), lambda i: (i, 0)),
                   pl.BlockSpec((tm, W_G), lambda i: (i, 0))],
        out_shape=[jax.ShapeDtypeStruct((n, W_A), F32),
                   jax.ShapeDtypeStruct((n, W_B), F32),
                   jax.ShapeDtypeStruct((n, W_G), F32)],
        compiler_params=_params("parallel"),
        name="inproj",
    )(x2, g, w)


def _attend_pair(q2, parts, bias_of):
    lane = lax.broadcasted_iota(jnp.int32, (1, LANES), 1)
    out = None
    for sub in range(2):
        m = (lane >= HEAD_DIM_A) if sub else (lane < HEAD_DIM_A)
        qm = jnp.where(m, q2 * ATTN_SCALE, 0.0).astype(BF16)
        scores = []
        for pi, (k2, _) in enumerate(parts):
            s = lax.dot_general(qm, k2, (((1,), (1,)), ((), ())), preferred_element_type=F32)
            scores.append(s + bias_of(sub, pi))
        mx = functools.reduce(jnp.maximum, [jnp.max(s, axis=-1, keepdims=True) for s in scores])
        l = None
        o = None
        for s, (_, v2) in zip(scores, parts):
            p = jnp.exp(s - mx)
            ls = jnp.sum(p, axis=-1, keepdims=True)
            os_ = jnp.dot(p.astype(BF16), v2, preferred_element_type=F32)
            l = ls if l is None else l + ls
            o = os_ if o is None else o + os_
        o = o / l
        out = o if out is None else jnp.where(m, o, out)
    return out


def _attn_prompt_body(q_ref, k_ref, v_ref, bias_ref, o_ref):
    qi = pl.program_id(1)
    win = Q_BLOCK + ATTN_REACH

    def run(kstart, nk, boff):
        for hp in range(N_HEADS_A // 2):
            ls = slice(hp * LANES, (hp + 1) * LANES)
            k2 = k_ref[0, pl.ds(kstart, nk), ls].astype(BF16)
            v2 = v_ref[0, pl.ds(kstart, nk), ls].astype(BF16)
            o_ref[0, :, ls] = _attend_pair(
                q_ref[0, :, ls], [(k2, v2)],
                lambda sub, pi: bias_ref[2 * hp + sub, :, boff:boff + nk])

    n_short = ATTN_REACH // Q_BLOCK
    for j in range(n_short):
        nk = (j + 1) * Q_BLOCK
        pl.when(qi == j)(functools.partial(run, 0, nk, win - nk))

    @pl.when(qi >= n_short)
    def _():
        run(pl.multiple_of(qi * Q_BLOCK - ATTN_REACH, Q_BLOCK), win, 0)


def _attn_prompt(za3, bias):
    b, s, _ = za3.shape
    assert s % Q_BLOCK == 0 and ATTN_REACH % Q_BLOCK == 0
    return pl.pallas_call(
        _attn_prompt_body,
        grid=(b, s // Q_BLOCK),
        in_specs=[pl.BlockSpec((1, Q_BLOCK, D_ATTN), lambda i, j: (i, j, 0)),
                  pl.BlockSpec((1, s, D_ATTN), lambda i, j: (i, 0, 1)),
                  pl.BlockSpec((1, s, D_ATTN), lambda i, j: (i, 0, 2)),
                  _const_spec(bias.shape)],
        out_specs=pl.BlockSpec((1, Q_BLOCK, D_ATTN), lambda i, j: (i, j, 0)),
        out_shape=jax.ShapeDtypeStruct((b, s, D_ATTN), F32),
        compiler_params=_params("parallel", "arbitrary"),
        name="attn_prompt",
    )(za3, za3, za3, bias)


def _attn_step_body(q_ref, kn_ref, vn_ref, kc_ref, vc_ref, bc_ref, bn_ref, o_ref):
    for hp in range(N_HEADS_A // 2):
        ls = slice(hp * LANES, (hp + 1) * LANES)
        parts = [(kc_ref[0, :, ls].astype(BF16), vc_ref[0, :, ls].astype(BF16)),
                 (kn_ref[0, :, ls].astype(BF16), vn_ref[0, :, ls].astype(BF16))]
        o_ref[0, :, ls] = _attend_pair(
            q_ref[0, :, ls], parts,
            lambda sub, pi: (bc_ref, bn_ref)[pi][2 * hp + sub])


def _attn_step(za3, kc3, vc3, bias_c, bias_n):
    b, t, _ = za3.shape
    past = kc3.shape[1]
    return pl.pallas_call(
        _attn_step_body,
        grid=(b,),
        in_specs=[pl.BlockSpec((1, t, D_ATTN), lambda i: (i, 0, 0)),
                  pl.BlockSpec((1, t, D_ATTN), lambda i: (i, 0, 1)),
                  pl.BlockSpec((1, t, D_ATTN), lambda i: (i, 0, 2)),
                  pl.BlockSpec((1, past, D_ATTN), lambda i: (i, 0, 0)),
                  pl.BlockSpec((1, past, D_ATTN), lambda i: (i, 0, 0)),
                  _const_spec(bias_c.shape),
                  _const_spec(bias_n.shape)],
        out_specs=pl.BlockSpec((1, t, D_ATTN), lambda i: (i, 0, 0)),
        out_shape=jax.ShapeDtypeStruct((b, t, D_ATTN), F32),
        compiler_params=_params("parallel"),
        name="attn_step",
    )(za3, za3, za3, kc3, vc3, bias_c, bias_n)


def _toeplitz_bias(rel_bias, tq, nk, past):
    span = tq + nk - 1
    lo = past - (nk - 1) + REL_CLIP
    padl = max(0, -lo)
    padr = max(0, lo + span - 1 - 2 * REL_CLIP)
    ext = jnp.pad(rel_bias.astype(F32), ((0, 0), (padl, padr)), mode="edge")
    g = ext[:, lo + padl:lo + padl + span][:, ::-1]
    flat = jnp.tile(g, (1, tq))[:, tq - 1:tq - 1 + tq * (span - 1)]
    return flat.reshape(-1, tq, span - 1)[:, :, :nk]


def _rel_bias_tables(rel_bias):
    win = Q_BLOCK + ATTN_REACH
    t = jnp.arange(Q_BLOCK)[:, None] // CHUNK
    j = jnp.arange(win)[None, :] // CHUNK
    band = (j >= t) & (j <= t + N_PAST_CHUNKS)
    return jnp.where(band[None], _toeplitz_bias(rel_bias, Q_BLOCK, win, ATTN_REACH), MASKED)


def _step_bias_tables(rel_bias, past, t_new):
    full = _toeplitz_bias(rel_bias, t_new, past + t_new, past)
    return full[:, :, :past], full[:, :, past:]


HGRN_ROWS = CHUNK
HGRN_STREAMS = 4


def _cumsum_rows(x):
    r = x.shape[0]
    assert r & (r - 1) == 0
    hi = x.astype(BF16)
    rest = x - hi.astype(F32)
    mid = rest.astype(BF16)
    lo = (rest - mid.astype(F32)).astype(BF16)
    row = lax.broadcasted_iota(jnp.int32, (r, 3 * r), 0)
    col = lax.broadcasted_iota(jnp.int32, (r, 3 * r), 1)
    tri = jnp.where((col & (r - 1)) <= row, 1.0, 0.0).astype(BF16)
    return jnp.dot(tri, jnp.concatenate([hi, mid, lo], axis=0), preferred_element_type=F32)


def _hgrn_chunk(cum, kk, iv, qr, g, hgh, st_t):
    rows = cum.shape[0]
    qq = qr * jax.nn.sigmoid(qr)
    vb = iv.astype(BF16)
    nsub = rows // SUB

    qe = (qq * jnp.exp(cum)).astype(BF16)
    o = lax.dot_general(qe, st_t.astype(BF16), (((1,), (1,)), ((), ())), preferred_element_type=F32)

    off = [jnp.zeros((SUB, DV_B), F32)]
    for bi in range(1, nsub):
        lo = bi * SUB
        ref = cum[lo - 1:lo, :]
        a = (qq[lo:lo + SUB] * jnp.exp(cum[lo:lo + SUB] - ref)).astype(BF16)
        bm = (kk[0:lo] * jnp.exp(ref - cum[0:lo])).astype(BF16)
        att = lax.dot_general(a, bm, (((1,), (1,)), ((), ())), preferred_element_type=F32)
        off.append(jnp.dot(att.astype(BF16), vb[0:lo], preferred_element_type=F32))
    o = o + jnp.concatenate(off, axis=0)

    rmod = lax.broadcasted_iota(jnp.int32, (rows, DK_B), 0) % SUB
    for d in range(SUB):
        ks, cs, vs = (kk, cum, iv) if d == 0 else (
            pltpu.roll(kk, d, axis=0), pltpu.roll(cum, d, axis=0), pltpu.roll(iv, d, axis=0))
        p = jnp.where(rmod >= d, qq * ks * jnp.exp(cum - cs), 0.0)
        o = o + jnp.sum(p, axis=-1, keepdims=True) * vs

    last = cum[rows - 1:rows, :]
    kd = (kk * jnp.exp(last - cum)).astype(BF16)
    st_new = st_t * jnp.exp(last) + jnp.dot(iv.T.astype(BF16), kd, preferred_element_type=F32)

    ob = o * lax.rsqrt(jnp.mean(o * o, axis=-1, keepdims=True) + EPS) * hgh
    ob = ob * (g * jax.nn.sigmoid(g))
    return ob, st_new


def _hgrn_body(*refs, has_s0):
    if has_s0:
        f_ref, i_ref, q_ref, g_ref, lb_ref, hg_ref, s0_ref, o_ref, sn_ref, st_ref = refs
    else:
        f_ref, i_ref, q_ref, g_ref, lb_ref, hg_ref, o_ref, sn_ref, st_ref = refs
    c = pl.program_id(1)
    streams = f_ref.shape[0]

    @pl.when(c == 0)
    def _():
        for b in range(streams):
            for h in range(N_HEADS_B):
                st_ref[b, h] = s0_ref[b, h].T if has_s0 else jnp.zeros((DV_B, DK_B), F32)

    lb = lb_ref[...]
    for b in range(streams):
        fg = lb + (1.0 - lb) * jax.nn.sigmoid(f_ref[b])
        cum = _cumsum_rows(jnp.log(fg))
        kk = 1.0 - fg
        for h in range(N_HEADS_B):
            ls = slice(h * LANES, (h + 1) * LANES)
            ob, st_new = _hgrn_chunk(cum[:, ls], kk[:, ls], i_ref[b, :, ls], q_ref[b, :, ls],
                                     g_ref[b, :, ls], hg_ref[:, ls], st_ref[b, h])
            o_ref[b, :, ls] = ob
            st_ref[b, h] = st_new

    @pl.when(c == pl.num_programs(1) - 1)
    def _():
        for b in range(streams):
            for h in range(N_HEADS_B):
                sn_ref[b, h] = st_ref[b, h].T


def _hgrn(zb3, lb, hg, s0):
    b, s, _ = zb3.shape
    rows = min(HGRN_ROWS, s)
    nb = HGRN_STREAMS
    assert s % rows == 0 and rows % SUB == 0 and b % nb == 0
    col = lambda k: pl.BlockSpec((nb, rows, D_RNN), lambda i, j: (i, j, k))
    in_specs = [col(0), col(1), col(2), col(3), _const_spec((1, D_RNN)), _const_spec((1, D_RNN))]
    args = [zb3, zb3, zb3, zb3, lb, hg]
    if s0 is not None:
        in_specs.append(pl.BlockSpec((nb, N_HEADS_B, DK_B, DV_B), lambda i, j: (i, 0, 0, 0)))
        args.append(s0)
    return pl.pallas_call(
        functools.partial(_hgrn_body, has_s0=s0 is not None),
        grid=(b // nb, s // rows),
        in_specs=in_specs,
        out_specs=[pl.BlockSpec((nb, rows, D_RNN), lambda i, j: (i, j, 0)),
                   pl.BlockSpec((nb, N_HEADS_B, DK_B, DV_B), lambda i, j: (i, 0, 0, 0))],
        out_shape=[jax.ShapeDtypeStruct((b, s, D_RNN), F32),
                   jax.ShapeDtypeStruct((b, N_HEADS_B, DK_B, DV_B), F32)],
        scratch_shapes=[pltpu.VMEM((nb, N_HEADS_B, DV_B, DK_B), F32)],
        compiler_params=_params("parallel", "arbitrary"),
        name="hgrn",
    )(*args)


def _finish_body(x_ref, oa_ref, ob_ref, zg_ref, wa_ref, wb_ref, wo_ref, h_ref):
    ma = jnp.dot(oa_ref[...].astype(BF16), wa_ref[...], preferred_element_type=F32)
    mb = jnp.dot(ob_ref[...].astype(BF16), wb_ref[...], preferred_element_type=F32)
    merged = (jax.nn.sigmoid(zg_ref[:, 0:D_MODEL]) * ma
              + jax.nn.sigmoid(zg_ref[:, D_MODEL:2 * D_MODEL]) * mb)
    h_ref[...] = x_ref[...] + jnp.dot(merged.astype(BF16), wo_ref[...], preferred_element_type=F32)


def _finish(x2, oa2, ob2, zg, wa, wb, wo):
    n = x2.shape[0]
    tm = _row_block(n, 512)
    row = lambda w: pl.BlockSpec((tm, w), lambda i: (i, 0))
    return pl.pallas_call(
        _finish_body,
        grid=(n // tm,),
        in_specs=[row(D_MODEL), row(D_ATTN), row(D_RNN), row(W_G),
                  _const_spec(wa.shape), _const_spec(wb.shape), _const_spec(wo.shape)],
        out_specs=row(D_MODEL),
        out_shape=jax.ShapeDtypeStruct((n, D_MODEL), F32),
        compiler_params=_params("parallel"),
        name="finish",
    )(x2, oa2, ob2, zg, wa, wb, wo)


N_TOP = PEER_TOPK + 1
E_BLOCK = D_MODEL
ROWS_PER_STEP = E_BLOCK // N_KEYS
N_SLOTS = 2
ROW_GROUP = 8


NO_RANK = 64


def _top_rows(s, n, want_rank=False):
    rows = []
    rank = jnp.full(s.shape, float(NO_RANK), F32) if want_rank else None
    for r in range(n):
        mx = jnp.max(s, axis=0, keepdims=True)
        rows.append(mx)
        hit = s == mx
        if want_rank:
            rank = jnp.where(hit, float(r), rank)
        if r + 1 < n:
            s = jnp.where(hit, -jnp.inf, s)
    return rows, rank


def _peer_select(hd, qt_ref, sk_ref, rk_ref, cn_ref, e1_ref, e2_ref):
    t = qt_ref.shape[1]
    sc = []
    for p in range(2):
        qhp = qt_ref[pl.ds(pl.multiple_of((hd * 2 + p) * PEER_HALF, PEER_HALF), PEER_HALF), :]
        sc.append(jnp.dot(sk_ref[p, hd], qhp.astype(BF16), preferred_element_type=F32))
    s1, s2 = sc
    a, _ = _top_rows(s1, N_TOP)
    b, rank2 = _top_rows(s2, N_TOP, want_rank=True)
    pad = 8 * ((N_TOP + 7) // 8)
    b_all = jnp.concatenate(b + [jnp.full((pad - N_TOP, t), -jnp.inf, F32)], axis=0)
    cand = jnp.concatenate([a[0] + b_all] + [a[p] + b_all[0:8] for p in range(1, N_TOP)], axis=0)
    c, _ = _top_rows(cand, N_TOP)
    tau = 0.5 * (c[PEER_TOPK - 1] + c[PEER_TOPK])
    z = functools.reduce(lambda x, y: x + y, [jnp.exp(cr - c[0]) for cr in c[:PEER_TOPK]])
    th = tau - s1
    cn_ref[hd] = functools.reduce(lambda x, y: x + y, [jnp.where(bq >= th, 1.0, 0.0) for bq in b])
    e1_ref[hd] = jnp.exp(s1 - a[0]) * (0.5 / z)
    rk_ref[hd] = rank2.astype(BF16)
    e2_ref[hd] = jnp.exp(s2 - b[0]).astype(BF16)


def _peer_body(h_ref, gf_ref, wq_ref, sk_ref, u_hbm, vt_hbm, gl_ref, y_ref,
               xt_ref, qt_ref, rk_ref, cn_ref, e1_ref, e2_ref, hid0, hid1, wg0, wg1, acc_ref,
               u_buf, vt_buf, u_sem, vt_sem):
    ne = u_hbm.shape[0]
    hid_refs, wg_refs = (hid0, hid1), (wg0, wg1)

    def u_copy(e, slot):
        return pltpu.make_async_copy(u_hbm.at[e], u_buf.at[slot], u_sem.at[slot])

    def vt_copy(e, slot):
        return pltpu.make_async_copy(vt_hbm.at[e], vt_buf.at[slot], vt_sem.at[slot])

    def start_if_in_range(copy_of, e, slot):
        if isinstance(e, int):
            if e < ne:
                copy_of(e, slot).start()
        else:
            pl.when(e < ne)(lambda: copy_of(e, slot).start())

    def stage_a(rows, slot, par):
        hid_refs[par][rows, :] = jnp.dot(u_buf[slot, rows, :], xt_ref[...], preferred_element_type=F32)

    def stage_b(i, rows, par):
        w = None
        for hd in range(PEER_HEADS):
            cnt = cn_ref[hd, pl.ds(i, 1), :].astype(BF16)
            e1 = e1_ref[hd, pl.ds(i, 1), :].astype(BF16)
            wh = jnp.where(rk_ref[hd] < cnt, e2_ref[hd], jnp.zeros((), BF16)) * e1
            w = wh if w is None else w + wh
        hid = hid_refs[par][rows, :]
        act = hid * (1.0 + lax.erf(hid * math.sqrt(0.5)))
        wg_refs[par][rows, :] = w * act.astype(BF16)

    def stage_c(rows, slot, par, first):
        d = jnp.dot(vt_buf[slot, rows, :], wg_refs[par][...], preferred_element_type=F32)
        if first:
            acc_ref[rows, :] = d
        else:
            acc_ref[rows, :] += d

    def iteration(e, par):
        static = isinstance(e, int)
        do_a = (not static) or e < ne
        do_b = (not static) or 1 <= e <= ne
        do_c = (not static) or e >= 2
        sa, sc = e % N_SLOTS, (e - 2) % N_SLOTS
        if do_a:
            u_copy(e, sa).wait()
        if do_c:
            vt_copy(e - 2, sc).wait()

        def row_group(g, carry):
            nrow = ROW_GROUP * N_KEYS
            rows = pl.ds(pl.multiple_of(g * nrow, nrow), nrow)
            if do_a:
                stage_a(rows, sa, par)
            if do_b:
                for k in range(ROW_GROUP):
                    r = g * ROW_GROUP + k
                    stage_b((e - 1) * ROWS_PER_STEP + r,
                            pl.ds(pl.multiple_of(r * N_KEYS, N_KEYS), N_KEYS), 1 - par)
            if do_c:
                stage_c(rows, sc, par, static and e == 2)
            return carry

        lax.fori_loop(0, ROWS_PER_STEP // ROW_GROUP, row_group, 0)
        if do_a:
            start_if_in_range(u_copy, e + N_SLOTS, sa)
        if do_c:
            start_if_in_range(vt_copy, e - 2 + N_SLOTS, sc)

    for s in range(N_SLOTS):
        u_copy(s, s).start()
        vt_copy(s, s).start()

    hn = _rmsnorm(h_ref[...], gf_ref[...])
    xt_ref[...] = hn.T.astype(BF16)
    qt_ref[...] = jnp.dot(wq_ref[...], xt_ref[...], preferred_element_type=F32)

    def head(hd, carry):
        _peer_select(hd, qt_ref, sk_ref, rk_ref, cn_ref, e1_ref, e2_ref)
        return carry

    lax.fori_loop(0, PEER_HEADS, head, 0)

    def steady(k, carry):
        iteration(2 * k, 0)
        iteration(2 * k + 1, 1)
        return carry

    assert ne % 2 == 0
    for e in range(4):
        iteration(e, e % 2)
    lax.fori_loop(2, ne // 2, steady, 0)
    for e in range(ne, ne + 2):
        iteration(e, e % 2)

    y_ref[...] = _rmsnorm(h_ref[...] + acc_ref[...].T, gl_ref[...])


def _peer(h2, g_ffn, wq_t, sk, u, v_t, g_last):
    n = h2.shape[0]
    t = _row_block(n, 512)
    assert u.shape[0] >= N_SLOTS + 2
    return pl.pallas_call(
        _peer_body,
        grid=(n // t,),
        in_specs=[pl.BlockSpec((t, D_MODEL), lambda i: (i, 0)),
                  _const_spec((1, D_MODEL)),
                  _const_spec(wq_t.shape),
                  _const_spec(sk.shape),
                  pl.BlockSpec(memory_space=pl.ANY),
                  pl.BlockSpec(memory_space=pl.ANY),
                  _const_spec((1, D_MODEL))],
        out_specs=pl.BlockSpec((t, D_MODEL), lambda i: (i, 0)),
        out_shape=jax.ShapeDtypeStruct((n, D_MODEL), F32),
        scratch_shapes=[pltpu.VMEM((D_MODEL, t), BF16),
                        pltpu.VMEM((PEER_HEADS * 2 * PEER_HALF, t), F32),
                        pltpu.VMEM((PEER_HEADS, N_KEYS, t), BF16),
                        pltpu.VMEM((PEER_HEADS, N_KEYS, t), F32),
                        pltpu.VMEM((PEER_HEADS, N_KEYS, t), F32),
                        pltpu.VMEM((PEER_HEADS, N_KEYS, t), BF16),
                        pltpu.VMEM((E_BLOCK, t), F32),
                        pltpu.VMEM((E_BLOCK, t), F32),
                        pltpu.VMEM((E_BLOCK, t), BF16),
                        pltpu.VMEM((E_BLOCK, t), BF16),
                        pltpu.VMEM((D_MODEL, t), F32),
                        pltpu.VMEM((N_SLOTS, E_BLOCK, D_MODEL), BF16),
                        pltpu.VMEM((N_SLOTS, D_MODEL, E_BLOCK), BF16),
                        pltpu.SemaphoreType.DMA((N_SLOTS,)),
                        pltpu.SemaphoreType.DMA((N_SLOTS,))],
        compiler_params=_params("parallel"),
        name="peer",
    )(h2, g_ffn, wq_t, sk, u, v_t, g_last)


def _layer(x3, s0, kc3, vc3, w, bias_tabs):
    b, s, _ = x3.shape
    x2 = x3.reshape(b * s, D_MODEL)
    za, zb, zg = _inproj(x2, w["norm_mix"], w["w_in"])
    za3 = za.reshape(b, s, W_A)
    if kc3 is None:
        oa = _attn_prompt(za3, bias_tabs)
    else:
        oa = _attn_step(za3, kc3, vc3, *bias_tabs)
    ob, s_new = _hgrn(zb.reshape(b, s, W_B), w["lb"], w["hgrn_norm"], s0)
    h2 = _finish(x2, oa.reshape(b * s, D_ATTN), ob.reshape(b * s, D_RNN), zg,
                 w["w_a"], w["w_b"], w["w_out"])
    k_rows = za3[:, :, D_ATTN:2 * D_ATTN]
    v_rows = za3[:, :, 2 * D_ATTN:3 * D_ATTN]
    return h2, k_rows, v_rows, s_new


def kernel(x_prompt, x_sample, cache_attn_k, cache_attn_v, state_hgrn, norm_mix, w_in, rel_bias,
           lb_logits, hgrn_norm, w_branch_a, w_branch_b, w_out, norm_ffn, peer_query, peer_subkeys,
           peer_u, peer_v, norm_final):
    depth = w_in.shape[0]
    bp, sp, _ = x_prompt.shape
    bs, ss, _ = x_sample.shape
    past = cache_attn_k.shape[2]
    keep = min(ATTN_REACH, sp)
    lb_all = jnp.cumsum(jax.nn.softmax(lb_logits.astype(F32), axis=0), axis=0)
    g_last = norm_final.reshape(1, D_MODEL)

    xp, xs = x_prompt, x_sample
    outs = [[] for _ in range(6)]
    for l in range(depth):
        w = dict(
            norm_mix=norm_mix[l].reshape(1, D_MODEL),
            w_in=w_in[l].astype(BF16),
            lb=lb_all[l].reshape(1, D_RNN),
            hgrn_norm=hgrn_norm[l].reshape(1, D_RNN),
            w_a=w_branch_a[l].astype(BF16),
            w_b=w_branch_b[l].astype(BF16),
            w_out=w_out[l].astype(BF16),
        )
        g_ffn = norm_ffn[l].reshape(1, D_MODEL)
        wq_t = peer_query[l].T.astype(BF16)
        sk = peer_subkeys[l].astype(BF16)
        ne = N_EXPERTS // E_BLOCK
        u = peer_u[l].astype(BF16).reshape(ne, E_BLOCK, D_MODEL)
        v_t = peer_v[l].astype(BF16).reshape(ne, E_BLOCK, D_MODEL).transpose(0, 2, 1)
        last = l == depth - 1

        hp, kp, vp, stp = _layer(xp, None, None, None, w, _rel_bias_tables(rel_bias[l]))
        hs, ks, vs, sts = _layer(
            xs, state_hgrn[l],
            cache_attn_k[l].reshape(bs, past, D_ATTN), cache_attn_v[l].reshape(bs, past, D_ATTN),
            w, _step_bias_tables(rel_bias[l], past, ss))
        assert last, "only the final layer fuses the closing rmsnorm"
        xp = _peer(hp, g_ffn, wq_t, sk, u, v_t, g_last).reshape(bp, sp, D_MODEL)
        xs = _peer(hs, g_ffn, wq_t, sk, u, v_t, g_last).reshape(bs, ss, D_MODEL)
        outs[0].append(kp[:, sp - keep:].reshape(bp, keep, N_HEADS_A, HEAD_DIM_A))
        outs[1].append(vp[:, sp - keep:].reshape(bp, keep, N_HEADS_A, HEAD_DIM_A))
        outs[2].append(stp)
        outs[3].append(ks.reshape(bs, ss, N_HEADS_A, HEAD_DIM_A))
        outs[4].append(vs.reshape(bs, ss, N_HEADS_A, HEAD_DIM_A))
        outs[5].append(sts)
    stacked = [jnp.stack(o, axis=0) for o in outs]
    return (xp, xs, *stacked)
```

```python
import functools
import math

import jax
import jax.numpy as jnp
from jax import lax
from jax.experimental import pallas as pl
from jax.experimental.pallas import tpu as pltpu

F32 = jnp.float32
BF16 = jnp.bfloat16

D_MODEL = 1024
CHUNK = 64
N_PAST_CHUNKS = 8
ATTN_REACH = N_PAST_CHUNKS * CHUNK
N_HEADS_A = 8
HEAD_DIM_A = 64
D_ATTN = N_HEADS_A * HEAD_DIM_A
REL_CLIP = 256
ATTN_SCALE = HEAD_DIM_A ** -0.5
N_HEADS_B = 4
DK_B = 128
DV_B = 128
D_RNN = N_HEADS_B * DK_B
PEER_HEADS = 8
N_KEYS = 128
N_EXPERTS = N_KEYS * N_KEYS
PEER_HALF = 128
PEER_TOPK = 16
EPS = 1e-6

LANES = 128
Q_BLOCK = 4 * CHUNK
SUB = 16
MASKED = -1e30
VMEM_LIMIT = 56 * 1024 * 1024


def _params(*sem):
    return pltpu.CompilerParams(dimension_semantics=sem, vmem_limit_bytes=VMEM_LIMIT)


def _const_spec(shape):
    nd = len(shape)
    return pl.BlockSpec(shape, lambda *_: (0,) * nd, pipeline_mode=pl.Buffered(1))


def _row_block(n, cap):
    t = cap
    while n % t:
        t //= 2
    assert t >= LANES, (n, cap)
    return t


def _rmsnorm(x, g):
    return x * lax.rsqrt(jnp.mean(x * x, axis=-1, keepdims=True) + EPS) * g


W_A = 3 * D_ATTN
W_B = 4 * D_RNN
W_G = 2 * D_MODEL


def _inproj_body(x_ref, g_ref, w_ref, za_ref, zb_ref, zg_ref):
    xn = _rmsnorm(x_ref[...], g_ref[...]).astype(BF16)
    za_ref[...] = jnp.dot(xn, w_ref[:, 0:W_A], preferred_element_type=F32)
    zb_ref[...] = jnp.dot(xn, w_ref[:, W_A:W_A + W_B], preferred_element_type=F32)
    zg_ref[...] = jnp.dot(xn, w_ref[:, W_A + W_B:W_A + W_B + W_G], preferred_element_type=F32).astype(BF16)


def _inproj(x2, g, w):
    n = x2.shape[0]
    tm = _row_block(n, 256)
    return pl.pallas_call(
        _inproj_body,
        grid=(n // tm,),
        in_specs=[pl.BlockSpec((tm, D_MODEL), lambda i: (i, 0)),
                  _const_spec((1, D_MODEL)),
                  _const_spec((D_MODEL, W_A + W_B + W_G))],
        out_specs=[pl.BlockSpec((tm, W_A), lambda i: (i, 0)),
                   pl.BlockSpec((tm, W_B), lambda i: (i, 0)),
                   pl.BlockSpec((tm, W_G), lambda i: (i, 0))],
        out_shape=[jax.ShapeDtypeStruct((n, W_A), F32),
                   jax.ShapeDtypeStruct((n, W_B), F32),
                   jax.ShapeDtypeStruct((n, W_G), BF16)],
        compiler_params=_params("parallel"),
        name="inproj",
    )(x2, g, w)


def _attend_pair(q2, parts, bias_of):
    lane = lax.broadcasted_iota(jnp.int32, (1, LANES), 1)
    out = None
    for sub in range(2):
        m = (lane >= HEAD_DIM_A) if sub else (lane < HEAD_DIM_A)
        qm = jnp.where(m, q2 * ATTN_SCALE, 0.0).astype(BF16)
        scores = []
        for pi, (k2, _) in enumerate(parts):
            s = lax.dot_general(qm, k2, (((1,), (1,)), ((), ())), preferred_element_type=F32)
            scores.append(s + bias_of(sub, pi))
        mx = functools.reduce(jnp.maximum, [jnp.max(s, axis=-1, keepdims=True) for s in scores])
        l = None
        o = None
        for s, (_, v2) in zip(scores, parts):
            p = jnp.exp(s - mx)
            ls = jnp.sum(p, axis=-1, keepdims=True)
            os_ = jnp.dot(p.astype(BF16), v2, preferred_element_type=F32)
            l = ls if l is None else l + ls
            o = os_ if o is None else o + os_
        o = o / l
        out = o if out is None else jnp.where(m, o, out)
    return out.astype(BF16)


def _attn_prompt_body(q_ref, k_ref, v_ref, bias_ref, o_ref):
    qi = pl.program_id(1)
    win = Q_BLOCK + ATTN_REACH

    def run(kstart, nk, boff):
        for hp in range(N_HEADS_A // 2):
            ls = slice(hp * LANES, (hp + 1) * LANES)
            k2 = k_ref[0, pl.ds(kstart, nk), ls].astype(BF16)
            v2 = v_ref[0, pl.ds(kstart, nk), ls].astype(BF16)
            o_ref[0, :, ls] = _attend_pair(
                q_ref[0, :, ls], [(k2, v2)],
                lambda sub, pi: bias_ref[2 * hp + sub, :, boff:boff + nk])

    n_short = ATTN_REACH // Q_BLOCK
    for j in range(n_short):
        nk = (j + 1) * Q_BLOCK
        pl.when(qi == j)(functools.partial(run, 0, nk, win - nk))

    @pl.when(qi >= n_short)
    def _():
        run(pl.multiple_of(qi * Q_BLOCK - ATTN_REACH, Q_BLOCK), win, 0)


def _attn_prompt(za3, bias):
    b, s, _ = za3.shape
    assert s % Q_BLOCK == 0 and ATTN_REACH % Q_BLOCK == 0
    return pl.pallas_call(
        _attn_prompt_body,
        grid=(b, s // Q_BLOCK),
        in_specs=[pl.BlockSpec((1, Q_BLOCK, D_ATTN), lambda i, j: (i, j, 0)),
                  pl.BlockSpec((1, s, D_ATTN), lambda i, j: (i, 0, 1)),
                  pl.BlockSpec((1, s, D_ATTN), lambda i, j: (i, 0, 2)),
                  _const_spec(bias.shape)],
        out_specs=pl.BlockSpec((1, Q_BLOCK, D_ATTN), lambda i, j: (i, j, 0)),
        out_shape=jax.ShapeDtypeStruct((b, s, D_ATTN), BF16),
        compiler_params=_params("parallel", "arbitrary"),
        name="attn_prompt",
    )(za3, za3, za3, bias)


def _attn_step_body(q_ref, kn_ref, vn_ref, kc_ref, vc_ref, bc_ref, bn_ref, o_ref):
    for hp in range(N_HEADS_A // 2):
        ls = slice(hp * LANES, (hp + 1) * LANES)
        parts = [(kc_ref[0, :, ls].astype(BF16), vc_ref[0, :, ls].astype(BF16)),
                 (kn_ref[0, :, ls].astype(BF16), vn_ref[0, :, ls].astype(BF16))]
        o_ref[0, :, ls] = _attend_pair(
            q_ref[0, :, ls], parts,
            lambda sub, pi: (bc_ref, bn_ref)[pi][2 * hp + sub])


def _attn_step(za3, kc3, vc3, bias_c, bias_n):
    b, t, _ = za3.shape
    past = kc3.shape[1]
    return pl.pallas_call(
        _attn_step_body,
        grid=(b,),
        in_specs=[pl.BlockSpec((1, t, D_ATTN), lambda i: (i, 0, 0)),
                  pl.BlockSpec((1, t, D_ATTN), lambda i: (i, 0, 1)),
                  pl.BlockSpec((1, t, D_ATTN), lambda i: (i, 0, 2)),
                  pl.BlockSpec((1, past, D_ATTN), lambda i: (i, 0, 0)),
                  pl.BlockSpec((1, past, D_ATTN), lambda i: (i, 0, 0)),
                  _const_spec(bias_c.shape),
                  _const_spec(bias_n.shape)],
        out_specs=pl.BlockSpec((1, t, D_ATTN), lambda i: (i, 0, 0)),
        out_shape=jax.ShapeDtypeStruct((b, t, D_ATTN), BF16),
        compiler_params=_params("parallel"),
        name="attn_step",
    )(za3, za3, za3, kc3, vc3, bias_c, bias_n)


def _toeplitz_bias(rel_bias, tq, nk, past):
    span = tq + nk - 1
    lo = past - (nk - 1) + REL_CLIP
    padl = max(0, -lo)
    padr = max(0, lo + span - 1 - 2 * REL_CLIP)
    ext = jnp.pad(rel_bias.astype(F32), ((0, 0), (padl, padr)), mode="edge")
    g = ext[:, lo + padl:lo + padl + span][:, ::-1]
    flat = jnp.tile(g, (1, tq))[:, tq - 1:tq - 1 + tq * (span - 1)]
    return flat.reshape(-1, tq, span - 1)[:, :, :nk]


def _rel_bias_tables(rel_bias):
    win = Q_BLOCK + ATTN_REACH
    t = jnp.arange(Q_BLOCK)[:, None] // CHUNK
    j = jnp.arange(win)[None, :] // CHUNK
    band = (j >= t) & (j <= t + N_PAST_CHUNKS)
    return jnp.where(band[None], _toeplitz_bias(rel_bias, Q_BLOCK, win, ATTN_REACH), MASKED)


def _step_bias_tables(rel_bias, past, t_new):
    full = _toeplitz_bias(rel_bias, t_new, past + t_new, past)
    return full[:, :, :past], full[:, :, past:]


HGRN_ROWS = CHUNK
HGRN_STREAMS = 4


def _cumsum_rows(x):
    r = x.shape[0]
    assert r & (r - 1) == 0
    hi = x.astype(BF16)
    rest = x - hi.astype(F32)
    mid = rest.astype(BF16)
    lo = (rest - mid.astype(F32)).astype(BF16)
    row = lax.broadcasted_iota(jnp.int32, (r, 3 * r), 0)
    col = lax.broadcasted_iota(jnp.int32, (r, 3 * r), 1)
    tri = jnp.where((col & (r - 1)) <= row, 1.0, 0.0).astype(BF16)
    return jnp.dot(tri, jnp.concatenate([hi, mid, lo], axis=0), preferred_element_type=F32)


def _hgrn_chunk(cum, kk, iv, qr, g, hgh, st_t):
    rows = cum.shape[0]
    qq = qr * jax.nn.sigmoid(qr)
    vb = iv.astype(BF16)
    nsub = rows // SUB

    qe = (qq * jnp.exp(cum)).astype(BF16)
    o = lax.dot_general(qe, st_t.astype(BF16), (((1,), (1,)), ((), ())), preferred_element_type=F32)

    off = [jnp.zeros((SUB, DV_B), F32)]
    for bi in range(1, nsub):
        lo = bi * SUB
        ref = cum[lo - 1:lo, :]
        a = (qq[lo:lo + SUB] * jnp.exp(cum[lo:lo + SUB] - ref)).astype(BF16)
        bm = (kk[0:lo] * jnp.exp(ref - cum[0:lo])).astype(BF16)
        att = lax.dot_general(a, bm, (((1,), (1,)), ((), ())), preferred_element_type=F32)
        off.append(jnp.dot(att.astype(BF16), vb[0:lo], preferred_element_type=F32))
    o = o + jnp.concatenate(off, axis=0)

    rmod = lax.broadcasted_iota(jnp.int32, (rows, DK_B), 0) % SUB
    for d in range(SUB):
        ks, cs, vs = (kk, cum, iv) if d == 0 else (
            pltpu.roll(kk, d, axis=0), pltpu.roll(cum, d, axis=0), pltpu.roll(iv, d, axis=0))
        p = jnp.where(rmod >= d, qq * ks * jnp.exp(cum - cs), 0.0)
        o = o + jnp.sum(p, axis=-1, keepdims=True) * vs

    last = cum[rows - 1:rows, :]
    kd = (kk * jnp.exp(last - cum)).astype(BF16)
    st_new = st_t * jnp.exp(last) + jnp.dot(iv.T.astype(BF16), kd, preferred_element_type=F32)

    ob = o * lax.rsqrt(jnp.mean(o * o, axis=-1, keepdims=True) + EPS) * hgh
    ob = ob * (g * jax.nn.sigmoid(g))
    return ob, st_new


def _hgrn_body(*refs, has_s0):
    if has_s0:
        f_ref, i_ref, q_ref, g_ref, lb_ref, hg_ref, s0_ref, o_ref, sn_ref, st_ref = refs
    else:
        f_ref, i_ref, q_ref, g_ref, lb_ref, hg_ref, o_ref, sn_ref, st_ref = refs
    c = pl.program_id(1)
    streams = f_ref.shape[0]

    @pl.when(c == 0)
    def _():
        for b in range(streams):
            for h in range(N_HEADS_B):
                st_ref[b, h] = s0_ref[b, h].T if has_s0 else jnp.zeros((DV_B, DK_B), F32)

    lb = lb_ref[...]
    for b in range(streams):
        fg = lb + (1.0 - lb) * jax.nn.sigmoid(f_ref[b])
        cum = _cumsum_rows(jnp.log(fg))
        kk = 1.0 - fg
        for h in range(N_HEADS_B):
            ls = slice(h * LANES, (h + 1) * LANES)
            ob, st_new = _hgrn_chunk(cum[:, ls], kk[:, ls], i_ref[b, :, ls], q_ref[b, :, ls],
                                     g_ref[b, :, ls], hg_ref[:, ls], st_ref[b, h])
            o_ref[b, :, ls] = ob.astype(BF16)
            st_ref[b, h] = st_new

    @pl.when(c == pl.num_programs(1) - 1)
    def _():
        for b in range(streams):
            for h in range(N_HEADS_B):
                sn_ref[b, h] = st_ref[b, h].T


def _hgrn(zb3, lb, hg, s0):
    b, s, _ = zb3.shape
    rows = min(HGRN_ROWS, s)
    nb = math.gcd(b, HGRN_STREAMS)
    assert s % rows == 0 and rows % SUB == 0
    col = lambda k: pl.BlockSpec((nb, rows, D_RNN), lambda i, j: (i, j, k))
    in_specs = [col(0), col(1), col(2), col(3), _const_spec((1, D_RNN)), _const_spec((1, D_RNN))]
    args = [zb3, zb3, zb3, zb3, lb, hg]
    if s0 is not None:
        in_specs.append(pl.BlockSpec((nb, N_HEADS_B, DK_B, DV_B), lambda i, j: (i, 0, 0, 0)))
        args.append(s0)
    return pl.pallas_call(
        functools.partial(_hgrn_body, has_s0=s0 is not None),
        grid=(b // nb, s // rows),
        in_specs=in_specs,
        out_specs=[pl.BlockSpec((nb, rows, D_RNN), lambda i, j: (i, j, 0)),
                   pl.BlockSpec((nb, N_HEADS_B, DK_B, DV_B), lambda i, j: (i, 0, 0, 0))],
        out_shape=[jax.ShapeDtypeStruct((b, s, D_RNN), BF16),
                   jax.ShapeDtypeStruct((b, N_HEADS_B, DK_B, DV_B), F32)],
        scratch_shapes=[pltpu.VMEM((nb, N_HEADS_B, DV_B, DK_B), F32)],
        compiler_params=_params("parallel", "arbitrary"),
        name="hgrn",
    )(*args)


def _finish_body(x_ref, oa_ref, ob_ref, zg_ref, wa_ref, wb_ref, wo_ref, h_ref):
    ma = jnp.dot(oa_ref[...], wa_ref[...], preferred_element_type=F32)
    mb = jnp.dot(ob_ref[...], wb_ref[...], preferred_element_type=F32)
    merged = (jax.nn.sigmoid(zg_ref[:, 0:D_MODEL].astype(F32)) * ma
              + jax.nn.sigmoid(zg_ref[:, D_MODEL:2 * D_MODEL].astype(F32)) * mb)
    h_ref[...] = x_ref[...] + jnp.dot(merged.astype(BF16), wo_ref[...], preferred_element_type=F32)


def _finish(x2, oa2, ob2, zg, wa, wb, wo):
    n = x2.shape[0]
    tm = _row_block(n, 512)
    row = lambda w: pl.BlockSpec((tm, w), lambda i: (i, 0))
    return pl.pallas_call(
        _finish_body,
        grid=(n // tm,),
        in_specs=[row(D_MODEL), row(D_ATTN), row(D_RNN), row(W_G),
                  _const_spec(wa.shape), _const_spec(wb.shape), _const_spec(wo.shape)],
        out_specs=row(D_MODEL),
        out_shape=jax.ShapeDtypeStruct((n, D_MODEL), F32),
        compiler_params=_params("parallel"),
        name="finish",
    )(x2, oa2, ob2, zg, wa, wb, wo)


N_TOP = PEER_TOPK + 1
E_BLOCK = D_MODEL
ROWS_PER_STEP = E_BLOCK // N_KEYS
N_SLOTS = 2
ROW_GROUP = 8


SLAB = 8


def _sort_network(n):
    pairs, p = [], 1
    while p < n:
        k = p
        while k >= 1:
            for j in range(k % p, n - k, 2 * k):
                for i in range(min(k, n - j - k)):
                    if (i + j) // (2 * p) == (i + j + k) // (2 * p):
                        pairs.append((i + j, i + j + k))
            k //= 2
        p *= 2
    return pairs


def _top_rows(slabs, n):
    v = list(slabs)
    width = 1
    while width < len(v):
        width *= 2
    for i, j in _sort_network(width):
        if j < len(v):
            v[i], v[j] = jnp.maximum(v[i], v[j]), jnp.minimum(v[i], v[j])
    rows = []
    for r in range(n):
        mx = jnp.max(v[0], axis=0, keepdims=True)
        rows.append(mx)
        if r + 1 < n:
            hit = v[0] == mx
            keep = min(len(v), n - 1 - r)
            v = [jnp.where(hit, v[k + 1] if k + 1 < len(v) else -jnp.inf, v[k]) for k in range(keep)]
    return rows


def _slabs(x):
    return [x[k * SLAB:(k + 1) * SLAB, :] for k in range(x.shape[0] // SLAB)]


def _peer_select(hd, qt_ref, sk_ref, rk_ref, cn_ref, e1_ref, e2_ref):
    t = qt_ref.shape[1]
    sc = []
    for p in range(2):
        qhp = qt_ref[pl.ds(pl.multiple_of((hd * 2 + p) * PEER_HALF, PEER_HALF), PEER_HALF), :]
        sc.append(jnp.dot(sk_ref[p, hd], qhp.astype(BF16), preferred_element_type=F32))
    for c in range(t // LANES):
        cols = slice(c * LANES, (c + 1) * LANES)
        _select_tile(hd, cols, sc[0][:, cols], sc[1][:, cols], rk_ref, cn_ref, e1_ref, e2_ref)


def _select_tile(hd, cols, s1, s2, rk_ref, cn_ref, e1_ref, e2_ref):
    t = s1.shape[1]
    a = _top_rows(_slabs(s1), N_TOP)
    b = _top_rows(_slabs(s2), N_TOP)
    ninf = jnp.full((1, t), -jnp.inf, F32)
    cand = [a[0] + jnp.concatenate(b[8 * k:8 * k + 8] + [ninf] * (8 * k + 8 - N_TOP), axis=0)
            for k in range((N_TOP + 7) // 8)]
    b8 = jnp.concatenate(b[0:8], axis=0)
    pq = [(p, q) for p in range(1, N_TOP) for q in range(N_TOP // (p + 1))]
    dense = [p for p in range(1, N_TOP) if N_TOP // (p + 1) > 4]
    cand += [a[p] + b8 for p in dense]
    rest = [(p, q) for p, q in pq if p not in dense]
    for k in range(0, len(rest), SLAB):
        grp = rest[k:k + SLAB]
        rows = [a[p] + b[q] for p, q in grp] + [ninf] * (SLAB - len(grp))
        cand.append(jnp.concatenate(rows, axis=0))
    c = _top_rows(cand, N_TOP)
    tau = 0.5 * (c[PEER_TOPK - 1] + c[PEER_TOPK])
    z = functools.reduce(lambda x, y: x + y, [jnp.exp(cr - c[0]) for cr in c[:PEER_TOPK]])
    th = tau - s1
    count = lambda cond: functools.reduce(lambda x, y: x + y, [jnp.where(cq, 1.0, 0.0) for cq in cond])
    cn_ref[hd, :, cols] = count([bq >= th for bq in b])
    rk_ref[hd, :, cols] = count([bq > s2 for bq in b]).astype(BF16)
    e1_ref[hd, :, cols] = jnp.exp(s1 - a[0]) * (0.5 / z)
    e2_ref[hd, :, cols] = jnp.exp(s2 - b[0]).astype(BF16)


def _peer_body(h_ref, gf_ref, wq_ref, sk_ref, u_hbm, vt_hbm, gl_ref, y_ref,
               xt_ref, qt_ref, rk_ref, cn_ref, e1_ref, e2_ref, hid0, hid1, wg0, wg1, acc_ref,
               u_buf, vt_buf, u_sem, vt_sem):
    ne = u_hbm.shape[0]
    hid_refs, wg_refs = (hid0, hid1), (wg0, wg1)

    def u_copy(e, slot):
        return pltpu.make_async_copy(u_hbm.at[e], u_buf.at[slot], u_sem.at[slot])

    def vt_copy(e, slot):
        return pltpu.make_async_copy(vt_hbm.at[e], vt_buf.at[slot], vt_sem.at[slot])

    def start_if_in_range(copy_of, e, slot):
        if isinstance(e, int):
            if e < ne:
                copy_of(e, slot).start()
        else:
            pl.when(e < ne)(lambda: copy_of(e, slot).start())

    def stage_a(rows, slot, par):
        hid_refs[par][rows, :] = jnp.dot(u_buf[slot, rows, :], xt_ref[...], preferred_element_type=F32)

    def stage_b(i, rows, par):
        w = None
        for hd in range(PEER_HEADS):
            cnt = cn_ref[hd, pl.ds(i, 1), :].astype(BF16)
            e1 = e1_ref[hd, pl.ds(i, 1), :].astype(BF16)
            wh = jnp.where(rk_ref[hd] < cnt, e2_ref[hd], jnp.zeros((), BF16)) * e1
            w = wh if w is None else w + wh
        hid = hid_refs[par][rows, :]
        act = hid * (1.0 + lax.erf(hid * math.sqrt(0.5)))
        wg_refs[par][rows, :] = w * act.astype(BF16)

    def stage_c(rows, slot, par, first):
        d = jnp.dot(vt_buf[slot, rows, :], wg_refs[par][...], preferred_element_type=F32)
        if first:
            acc_ref[rows, :] = d
        else:
            acc_ref[rows, :] += d

    def iteration(e, par):
        static = isinstance(e, int)
        do_a = (not static) or e < ne
        do_b = (not static) or 1 <= e <= ne
        do_c = (not static) or e >= 2
        sa, sc = e % N_SLOTS, (e - 2) % N_SLOTS
        if do_a:
            u_copy(e, sa).wait()
        if do_c:
            vt_copy(e - 2, sc).wait()

        def row_group(g, carry):
            nrow = ROW_GROUP * N_KEYS
            rows = pl.ds(pl.multiple_of(g * nrow, nrow), nrow)
            if do_a:
                stage_a(rows, sa, par)
            if do_b:
                for k in range(ROW_GROUP):
                    r = g * ROW_GROUP + k
                    stage_b((e - 1) * ROWS_PER_STEP + r,
                            pl.ds(pl.multiple_of(r * N_KEYS, N_KEYS), N_KEYS), 1 - par)
            if do_c:
                stage_c(rows, sc, par, static and e == 2)
            return carry

        lax.fori_loop(0, ROWS_PER_STEP // ROW_GROUP, row_group, 0)
        if do_a:
            start_if_in_range(u_copy, e + N_SLOTS, sa)
        if do_c:
            start_if_in_range(vt_copy, e - 2 + N_SLOTS, sc)

    for s in range(N_SLOTS):
        u_copy(s, s).start()
        vt_copy(s, s).start()

    hn = _rmsnorm(h_ref[...], gf_ref[...])
    xt_ref[...] = hn.T.astype(BF16)
    qt_ref[...] = jnp.dot(wq_ref[...], xt_ref[...], preferred_element_type=F32)

    def head(hd, carry):
        _peer_select(hd, qt_ref, sk_ref, rk_ref, cn_ref, e1_ref, e2_ref)
        return carry

    lax.fori_loop(0, PEER_HEADS, head, 0)

    def steady(k, carry):
        iteration(2 * k, 0)
        iteration(2 * k + 1, 1)
        return carry

    assert ne % 2 == 0
    for e in range(4):
        iteration(e, e % 2)
    lax.fori_loop(2, ne // 2, steady, 0)
    for e in range(ne, ne + 2):
        iteration(e, e % 2)

    y_ref[...] = _rmsnorm(h_ref[...] + acc_ref[...].T, gl_ref[...])


def _peer(h2, g_ffn, wq_t, sk, u, v_t, g_last):
    n = h2.shape[0]
    t = _row_block(n, 512)
    assert u.shape[0] >= N_SLOTS + 2
    return pl.pallas_call(
        _peer_body,
        grid=(n // t,),
        in_specs=[pl.BlockSpec((t, D_MODEL), lambda i: (i, 0)),
                  _const_spec((1, D_MODEL)),
                  _const_spec(wq_t.shape),
                  _const_spec(sk.shape),
                  pl.BlockSpec(memory_space=pl.ANY),
                  pl.BlockSpec(memory_space=pl.ANY),
                  _const_spec((1, D_MODEL))],
        out_specs=pl.BlockSpec((t, D_MODEL), lambda i: (i, 0)),
        out_shape=jax.ShapeDtypeStruct((n, D_MODEL), F32),
        scratch_shapes=[pltpu.VMEM((D_MODEL, t), BF16),
                        pltpu.VMEM((PEER_HEADS * 2 * PEER_HALF, t), F32),
                        pltpu.VMEM((PEER_HEADS, N_KEYS, t), BF16),
                        pltpu.VMEM((PEER_HEADS, N_KEYS, t), F32),
                        pltpu.VMEM((PEER_HEADS, N_KEYS, t), F32),
                        pltpu.VMEM((PEER_HEADS, N_KEYS, t), BF16),
                        pltpu.VMEM((E_BLOCK, t), F32),
                        pltpu.VMEM((E_BLOCK, t), F32),
                        pltpu.VMEM((E_BLOCK, t), BF16),
                        pltpu.VMEM((E_BLOCK, t), BF16),
                        pltpu.VMEM((D_MODEL, t), F32),
                        pltpu.VMEM((N_SLOTS, E_BLOCK, D_MODEL), BF16),
                        pltpu.VMEM((N_SLOTS, D_MODEL, E_BLOCK), BF16),
                        pltpu.SemaphoreType.DMA((N_SLOTS,)),
                        pltpu.SemaphoreType.DMA((N_SLOTS,))],
        compiler_params=_params("parallel"),
        name="peer",
    )(h2, g_ffn, wq_t, sk, u, v_t, g_last)


def _layer(x3, s0, kc3, vc3, w, bias_tabs):
    b, s, _ = x3.shape
    x2 = x3.reshape(b * s, D_MODEL)
    za, zb, zg = _inproj(x2, w["norm_mix"], w["w_in"])
    za3 = za.reshape(b, s, W_A)
    if kc3 is None:
        oa = _attn_prompt(za3, bias_tabs)
    else:
        oa = _attn_step(za3, kc3, vc3, *bias_tabs)
    ob, s_new = _hgrn(zb.reshape(b, s, W_B), w["lb"], w["hgrn_norm"], s0)
    h2 = _finish(x2, oa.reshape(b * s, D_ATTN), ob.reshape(b * s, D_RNN), zg,
                 w["w_a"], w["w_b"], w["w_out"])
    k_rows = za3[:, :, D_ATTN:2 * D_ATTN]
    v_rows = za3[:, :, 2 * D_ATTN:3 * D_ATTN]
    return h2, k_rows, v_rows, s_new


def kernel(x_prompt, x_sample, cache_attn_k, cache_attn_v, state_hgrn, norm_mix, w_in, rel_bias,
           lb_logits, hgrn_norm, w_branch_a, w_branch_b, w_out, norm_ffn, peer_query, peer_subkeys,
           peer_u, peer_v, norm_final):
    depth = w_in.shape[0]
    bp, sp, _ = x_prompt.shape
    bs, ss, _ = x_sample.shape
    past = cache_attn_k.shape[2]
    keep = min(ATTN_REACH, sp)
    lb_all = jnp.cumsum(jax.nn.softmax(lb_logits.astype(F32), axis=0), axis=0)
    g_last = norm_final.reshape(1, D_MODEL)

    xp, xs = x_prompt, x_sample
    outs = [[] for _ in range(6)]
    for l in range(depth):
        w = dict(
            norm_mix=norm_mix[l].reshape(1, D_MODEL),
            w_in=w_in[l].astype(BF16),
            lb=lb_all[l].reshape(1, D_RNN),
            hgrn_norm=hgrn_norm[l].reshape(1, D_RNN),
            w_a=w_branch_a[l].astype(BF16),
            w_b=w_branch_b[l].astype(BF16),
            w_out=w_out[l].astype(BF16),
        )
        g_ffn = norm_ffn[l].reshape(1, D_MODEL)
        wq_t = peer_query[l].T.astype(BF16)
        sk = peer_subkeys[l].astype(BF16)
        ne = N_EXPERTS // E_BLOCK
        u = peer_u[l].astype(BF16).reshape(ne, E_BLOCK, D_MODEL)
        v_t = peer_v[l].astype(BF16).reshape(ne, E_BLOCK, D_MODEL).transpose(0, 2, 1)
        last = l == depth - 1

        hp, kp, vp, stp = _layer(xp, None, None, None, w, _rel_bias_tables(rel_bias[l]))
        hs, ks, vs, sts = _layer(
            xs, state_hgrn[l],
            cache_attn_k[l].reshape(bs, past, D_ATTN), cache_attn_v[l].reshape(bs, past, D_ATTN),
            w, _step_bias_tables(rel_bias[l], past, ss))
        assert last, "only the final layer fuses the closing rmsnorm"
        xp = _peer(hp, g_ffn, wq_t, sk, u, v_t, g_last).reshape(bp, sp, D_MODEL)
        xs = _peer(hs, g_ffn, wq_t, sk, u, v_t, g_last).reshape(bs, ss, D_MODEL)
        outs[0].append(kp[:, sp - keep:].reshape(bp, keep, N_HEADS_A, HEAD_DIM_A))
        outs[1].append(vp[:, sp - keep:].reshape(bp, keep, N_HEADS_A, HEAD_DIM_A))
        outs[2].append(stp)
        outs[3].append(ks.reshape(bs, ss, N_HEADS_A, HEAD_DIM_A))
        outs[4].append(vs.reshape(bs, ss, N_HEADS_A, HEAD_DIM_A))
        outs[5].append(sts)
    stacked = [jnp.stack(o, axis=0) for o in outs]
    return (xp, xs, *stacked)
```

```python
import functools
import math

import jax
import jax.numpy as jnp
from jax import lax
from jax.experimental import pallas as pl
from jax.experimental.pallas import tpu as pltpu

F32 = jnp.float32
BF16 = jnp.bfloat16

D_MODEL = 1024
CHUNK = 64
N_PAST_CHUNKS = 8
ATTN_REACH = N_PAST_CHUNKS * CHUNK
N_HEADS_A = 8
HEAD_DIM_A = 64
D_ATTN = N_HEADS_A * HEAD_DIM_A
REL_CLIP = 256
ATTN_SCALE = HEAD_DIM_A ** -0.5
N_HEADS_B = 4
DK_B = 128
DV_B = 128
D_RNN = N_HEADS_B * DK_B
PEER_HEADS = 8
N_KEYS = 128
N_EXPERTS = N_KEYS * N_KEYS
PEER_HALF = 128
PEER_TOPK = 16
EPS = 1e-6

LANES = 128
Q_BLOCK = 4 * CHUNK
SUB = 8
MASKED = -1e30
VMEM_LIMIT = 56 * 1024 * 1024


def _params(*sem):
    return pltpu.CompilerParams(dimension_semantics=sem, vmem_limit_bytes=VMEM_LIMIT)


def _const_spec(shape):
    nd = len(shape)
    return pl.BlockSpec(shape, lambda *_: (0,) * nd, pipeline_mode=pl.Buffered(1))


def _row_block(n, cap):
    t = cap
    while n % t:
        t //= 2
    assert t >= LANES, (n, cap)
    return t


def _rmsnorm(x, g):
    return x * lax.rsqrt(jnp.mean(x * x, axis=-1, keepdims=True) + EPS) * g


W_A = 3 * D_ATTN
W_B = 4 * D_RNN
W_G = 2 * D_MODEL


def _inproj_body(x_ref, g_ref, w_ref, za_ref, zb_ref, zg_ref):
    xn = _rmsnorm(x_ref[...], g_ref[...]).astype(BF16)
    za_ref[...] = jnp.dot(xn, w_ref[:, 0:W_A], preferred_element_type=F32)
    zb_ref[...] = jnp.dot(xn, w_ref[:, W_A:W_A + W_B], preferred_element_type=F32)
    zg_ref[...] = jnp.dot(xn, w_ref[:, W_A + W_B:W_A + W_B + W_G], preferred_element_type=F32).astype(BF16)


def _inproj(x2, g, w):
    n = x2.shape[0]
    tm = _row_block(n, 512)
    return pl.pallas_call(
        _inproj_body,
        grid=(n // tm,),
        in_specs=[pl.BlockSpec((tm, D_MODEL), lambda i: (i, 0)),
                  _const_spec((1, D_MODEL)),
                  _const_spec((D_MODEL, W_A + W_B + W_G))],
        out_specs=[pl.BlockSpec((tm, W_A), lambda i: (i, 0)),
                   pl.BlockSpec((tm, W_B), lambda i: (i, 0)),
                   pl.BlockSpec((tm, W_G), lambda i: (i, 0))],
        out_shape=[jax.ShapeDtypeStruct((n, W_A), F32),
                   jax.ShapeDtypeStruct((n, W_B), F32),
                   jax.ShapeDtypeStruct((n, W_G), BF16)],
        compiler_params=_params("parallel"),
        name="inproj",
    )(x2, g, w)


def _attend_pair(q2, parts, bias_of):
    lane = lax.broadcasted_iota(jnp.int32, (1, LANES), 1)
    out = None
    for sub in range(2):
        m = (lane >= HEAD_DIM_A) if sub else (lane < HEAD_DIM_A)
        qm = jnp.where(m, q2 * ATTN_SCALE, 0.0).astype(BF16)
        scores = []
        for pi, (k2, _) in enumerate(parts):
            s = lax.dot_general(qm, k2, (((1,), (1,)), ((), ())), preferred_element_type=F32)
            scores.append(s + bias_of(sub, pi))
        mx = functools.reduce(jnp.maximum, [jnp.max(s, axis=-1, keepdims=True) for s in scores])
        l = None
        o = None
        for s, (_, v2) in zip(scores, parts):
            p = jnp.exp(s - mx)
            ls = jnp.sum(p, axis=-1, keepdims=True)
            os_ = jnp.dot(p.astype(BF16), v2, preferred_element_type=F32)
            l = ls if l is None else l + ls
            o = os_ if o is None else o + os_
        o = o / l
        out = o if out is None else jnp.where(m, o, out)
    return out.astype(BF16)


def _attn_prompt_body(q_ref, k_ref, v_ref, bias_ref, o_ref):
    qi = pl.program_id(1)
    win = Q_BLOCK + ATTN_REACH

    def run(kstart, nk, boff):
        for hp in range(N_HEADS_A // 2):
            ls = slice(hp * LANES, (hp + 1) * LANES)
            k2 = k_ref[0, pl.ds(kstart, nk), ls].astype(BF16)
            v2 = v_ref[0, pl.ds(kstart, nk), ls].astype(BF16)
            o_ref[0, :, ls] = _attend_pair(
                q_ref[0, :, ls], [(k2, v2)],
                lambda sub, pi: bias_ref[2 * hp + sub, :, boff:boff + nk])

    n_short = ATTN_REACH // Q_BLOCK
    for j in range(n_short):
        nk = (j + 1) * Q_BLOCK
        pl.when(qi == j)(functools.partial(run, 0, nk, win - nk))

    @pl.when(qi >= n_short)
    def _():
        run(pl.multiple_of(qi * Q_BLOCK - ATTN_REACH, Q_BLOCK), win, 0)


def _attn_prompt(za3, bias):
    b, s, _ = za3.shape
    assert s % Q_BLOCK == 0 and ATTN_REACH % Q_BLOCK == 0
    return pl.pallas_call(
        _attn_prompt_body,
        grid=(b, s // Q_BLOCK),
        in_specs=[pl.BlockSpec((1, Q_BLOCK, D_ATTN), lambda i, j: (i, j, 0)),
                  pl.BlockSpec((1, s, D_ATTN), lambda i, j: (i, 0, 1)),
                  pl.BlockSpec((1, s, D_ATTN), lambda i, j: (i, 0, 2)),
                  _const_spec(bias.shape)],
        out_specs=pl.BlockSpec((1, Q_BLOCK, D_ATTN), lambda i, j: (i, j, 0)),
        out_shape=jax.ShapeDtypeStruct((b, s, D_ATTN), BF16),
        compiler_params=_params("parallel", "arbitrary"),
        name="attn_prompt",
    )(za3, za3, za3, bias)


def _attn_step_body(q_ref, kn_ref, vn_ref, kc_ref, vc_ref, bc_ref, bn_ref, o_ref):
    for hp in range(N_HEADS_A // 2):
        ls = slice(hp * LANES, (hp + 1) * LANES)
        parts = [(kc_ref[0, :, ls].astype(BF16), vc_ref[0, :, ls].astype(BF16)),
                 (kn_ref[0, :, ls].astype(BF16), vn_ref[0, :, ls].astype(BF16))]
        o_ref[0, :, ls] = _attend_pair(
            q_ref[0, :, ls], parts,
            lambda sub, pi: (bc_ref, bn_ref)[pi][2 * hp + sub])


def _attn_step(za3, kc3, vc3, bias_c, bias_n):
    b, t, _ = za3.shape
    past = kc3.shape[1]
    return pl.pallas_call(
        _attn_step_body,
        grid=(b,),
        in_specs=[pl.BlockSpec((1, t, D_ATTN), lambda i: (i, 0, 0)),
                  pl.BlockSpec((1, t, D_ATTN), lambda i: (i, 0, 1)),
                  pl.BlockSpec((1, t, D_ATTN), lambda i: (i, 0, 2)),
                  pl.BlockSpec((1, past, D_ATTN), lambda i: (i, 0, 0)),
                  pl.BlockSpec((1, past, D_ATTN), lambda i: (i, 0, 0)),
                  _const_spec(bias_c.shape),
                  _const_spec(bias_n.shape)],
        out_specs=pl.BlockSpec((1, t, D_ATTN), lambda i: (i, 0, 0)),
        out_shape=jax.ShapeDtypeStruct((b, t, D_ATTN), BF16),
        compiler_params=_params("parallel"),
        name="attn_step",
    )(za3, za3, za3, kc3, vc3, bias_c, bias_n)


def _toeplitz_bias(rel_bias, tq, nk, past):
    span = tq + nk - 1
    lo = past - (nk - 1) + REL_CLIP
    padl = max(0, -lo)
    padr = max(0, lo + span - 1 - 2 * REL_CLIP)
    ext = jnp.pad(rel_bias.astype(F32), ((0, 0), (padl, padr)), mode="edge")
    g = ext[:, lo + padl:lo + padl + span][:, ::-1]
    flat = jnp.tile(g, (1, tq))[:, tq - 1:tq - 1 + tq * (span - 1)]
    return flat.reshape(-1, tq, span - 1)[:, :, :nk]


def _rel_bias_tables(rel_bias):
    win = Q_BLOCK + ATTN_REACH
    t = jnp.arange(Q_BLOCK)[:, None] // CHUNK
    j = jnp.arange(win)[None, :] // CHUNK
    band = (j >= t) & (j <= t + N_PAST_CHUNKS)
    return jnp.where(band[None], _toeplitz_bias(rel_bias, Q_BLOCK, win, ATTN_REACH), MASKED)


def _step_bias_tables(rel_bias, past, t_new):
    full = _toeplitz_bias(rel_bias, t_new, past + t_new, past)
    return full[:, :, :past], full[:, :, past:]


HGRN_ROWS = CHUNK
HGRN_STREAMS = 4


def _cumsum_rows(x):
    r = x.shape[0]
    assert r & (r - 1) == 0
    hi = x.astype(BF16)
    rest = x - hi.astype(F32)
    mid = rest.astype(BF16)
    lo = (rest - mid.astype(F32)).astype(BF16)
    row = lax.broadcasted_iota(jnp.int32, (r, 3 * r), 0)
    col = lax.broadcasted_iota(jnp.int32, (r, 3 * r), 1)
    tri = jnp.where((col & (r - 1)) <= row, 1.0, 0.0).astype(BF16)
    return jnp.dot(tri, jnp.concatenate([hi, mid, lo], axis=0), preferred_element_type=F32)


def _hgrn_chunk(cum, kk, iv, qr, g, hgh, st_t):
    rows = cum.shape[0]
    qq = qr * jax.nn.sigmoid(qr)
    vb = iv.astype(BF16)
    nsub = rows // SUB

    qe = (qq * jnp.exp(cum)).astype(BF16)
    o = lax.dot_general(qe, st_t.astype(BF16), (((1,), (1,)), ((), ())), preferred_element_type=F32)

    zeros = jnp.zeros((SUB, DK_B), F32)
    refs = [cum[bi * SUB - 1:bi * SUB, :] for bi in range(1, nsub)]
    a_all = qq[SUB:] * jnp.exp(cum[SUB:] - jnp.concatenate(
        [jnp.broadcast_to(r, (SUB, DK_B)) for r in refs], axis=0))
    a_cols, b_cols = [], []
    for bi in range(1, nsub):
        lo = bi * SUB
        a_cols.append(jnp.concatenate(
            [zeros] * bi + [a_all[lo - SUB:lo]] + [zeros] * (nsub - 1 - bi), axis=0))
        b_cols.append(jnp.concatenate(
            [kk[0:lo] * jnp.exp(refs[bi - 1] - cum[0:lo])] + [zeros] * (nsub - bi), axis=0))
    att = lax.dot_general(jnp.concatenate(a_cols, axis=1).astype(BF16),
                          jnp.concatenate(b_cols, axis=1).astype(BF16),
                          (((1,), (1,)), ((), ())), preferred_element_type=F32)
    o = o + jnp.dot(att.astype(BF16), vb, preferred_element_type=F32)

    rmod = lax.broadcasted_iota(jnp.int32, (rows, DK_B), 0) % SUB
    for d in range(SUB):
        ks, cs, vs = (kk, cum, iv) if d == 0 else (
            pltpu.roll(kk, d, axis=0), pltpu.roll(cum, d, axis=0), pltpu.roll(iv, d, axis=0))
        p = jnp.where(rmod >= d, qq * ks * jnp.exp(cum - cs), 0.0)
        o = o + jnp.sum(p, axis=-1, keepdims=True) * vs

    last = cum[rows - 1:rows, :]
    kd = (kk * jnp.exp(last - cum)).astype(BF16)
    st_new = st_t * jnp.exp(last) + jnp.dot(iv.T.astype(BF16), kd, preferred_element_type=F32)

    ob = o * lax.rsqrt(jnp.mean(o * o, axis=-1, keepdims=True) + EPS) * hgh
    ob = ob * (g * jax.nn.sigmoid(g))
    return ob, st_new


def _hgrn_body(*refs, has_s0):
    if has_s0:
        f_ref, i_ref, q_ref, g_ref, lb_ref, hg_ref, s0_ref, o_ref, sn_ref, st_ref = refs
    else:
        f_ref, i_ref, q_ref, g_ref, lb_ref, hg_ref, o_ref, sn_ref, st_ref = refs
    c = pl.program_id(1)
    streams = f_ref.shape[0]

    @pl.when(c == 0)
    def _():
        for b in range(streams):
            for h in range(N_HEADS_B):
                st_ref[b, h] = s0_ref[b, h].T if has_s0 else jnp.zeros((DV_B, DK_B), F32)

    lb = lb_ref[...]
    for b in range(streams):
        fg = lb + (1.0 - lb) * jax.nn.sigmoid(f_ref[b])
        cum = _cumsum_rows(jnp.log(fg))
        kk = 1.0 - fg
        for h in range(N_HEADS_B):
            ls = slice(h * LANES, (h + 1) * LANES)
            ob, st_new = _hgrn_chunk(cum[:, ls], kk[:, ls], i_ref[b, :, ls], q_ref[b, :, ls],
                                     g_ref[b, :, ls], hg_ref[:, ls], st_ref[b, h])
            o_ref[b, :, ls] = ob.astype(BF16)
            st_ref[b, h] = st_new

    @pl.when(c == pl.num_programs(1) - 1)
    def _():
        for b in range(streams):
            for h in range(N_HEADS_B):
                sn_ref[b, h] = st_ref[b, h].T


def _hgrn(zb3, lb, hg, s0):
    b, s, _ = zb3.shape
    rows = min(HGRN_ROWS, s)
    nb = math.gcd(b, HGRN_STREAMS)
    assert s % rows == 0 and rows % SUB == 0
    col = lambda k: pl.BlockSpec((nb, rows, D_RNN), lambda i, j: (i, j, k))
    in_specs = [col(0), col(1), col(2), col(3), _const_spec((1, D_RNN)), _const_spec((1, D_RNN))]
    args = [zb3, zb3, zb3, zb3, lb, hg]
    if s0 is not None:
        in_specs.append(pl.BlockSpec((nb, N_HEADS_B, DK_B, DV_B), lambda i, j: (i, 0, 0, 0)))
        args.append(s0)
    return pl.pallas_call(
        functools.partial(_hgrn_body, has_s0=s0 is not None),
        grid=(b // nb, s // rows),
        in_specs=in_specs,
        out_specs=[pl.BlockSpec((nb, rows, D_RNN), lambda i, j: (i, j, 0)),
                   pl.BlockSpec((nb, N_HEADS_B, DK_B, DV_B), lambda i, j: (i, 0, 0, 0))],
        out_shape=[jax.ShapeDtypeStruct((b, s, D_RNN), BF16),
                   jax.ShapeDtypeStruct((b, N_HEADS_B, DK_B, DV_B), F32)],
        scratch_shapes=[pltpu.VMEM((nb, N_HEADS_B, DV_B, DK_B), F32)],
        compiler_params=_params("parallel", "arbitrary"),
        name="hgrn",
    )(*args)


def _finish_body(x_ref, oa_ref, ob_ref, zg_ref, wa_ref, wb_ref, wo_ref, h_ref):
    ma = jnp.dot(oa_ref[...], wa_ref[...], preferred_element_type=F32)
    mb = jnp.dot(ob_ref[...], wb_ref[...], preferred_element_type=F32)
    merged = (jax.nn.sigmoid(zg_ref[:, 0:D_MODEL].astype(F32)) * ma
              + jax.nn.sigmoid(zg_ref[:, D_MODEL:2 * D_MODEL].astype(F32)) * mb)
    h_ref[...] = x_ref[...] + jnp.dot(merged.astype(BF16), wo_ref[...], preferred_element_type=F32)


def _finish(x2, oa2, ob2, zg, wa, wb, wo):
    n = x2.shape[0]
    tm = _row_block(n, 512)
    row = lambda w: pl.BlockSpec((tm, w), lambda i: (i, 0))
    return pl.pallas_call(
        _finish_body,
        grid=(n // tm,),
        in_specs=[row(D_MODEL), row(D_ATTN), row(D_RNN), row(W_G),
                  _const_spec(wa.shape), _const_spec(wb.shape), _const_spec(wo.shape)],
        out_specs=row(D_MODEL),
        out_shape=jax.ShapeDtypeStruct((n, D_MODEL), F32),
        compiler_params=_params("parallel"),
        name="finish",
    )(x2, oa2, ob2, zg, wa, wb, wo)


N_TOP = PEER_TOPK + 1
E_BLOCK = D_MODEL
ROWS_PER_STEP = E_BLOCK // N_KEYS
N_SLOTS = 2
ROW_GROUP = 8


SLAB = 8


def _sort_network(n):
    pairs, p = [], 1
    while p < n:
        k = p
        while k >= 1:
            for j in range(k % p, n - k, 2 * k):
                for i in range(min(k, n - j - k)):
                    if (i + j) // (2 * p) == (i + j + k) // (2 * p):
                        pairs.append((i + j, i + j + k))
            k //= 2
        p *= 2
    return pairs


def _top_rows(slabs, n):
    v = list(slabs)
    width = 1
    while width < len(v):
        width *= 2
    for i, j in _sort_network(width):
        if j < len(v):
            v[i], v[j] = jnp.maximum(v[i], v[j]), jnp.minimum(v[i], v[j])
    rows = []
    for r in range(n):
        mx = jnp.max(v[0], axis=0, keepdims=True)
        rows.append(mx)
        if r + 1 < n:
            hit = v[0] == mx
            keep = min(len(v), n - 1 - r)
            v = [jnp.where(hit, v[k + 1] if k + 1 < len(v) else -jnp.inf, v[k]) for k in range(keep)]
    return rows


def _slabs(x):
    return [x[k * SLAB:(k + 1) * SLAB, :] for k in range(x.shape[0] // SLAB)]


def _peer_select(hd, qt_ref, sk_ref, rk_ref, cn_ref, e1_ref, e2_ref):
    t = qt_ref.shape[1]
    sc = []
    for p in range(2):
        qhp = qt_ref[pl.ds(pl.multiple_of((hd * 2 + p) * PEER_HALF, PEER_HALF), PEER_HALF), :]
        sc.append(jnp.dot(sk_ref[p, hd], qhp.astype(BF16), preferred_element_type=F32))
    for c in range(t // LANES):
        cols = slice(c * LANES, (c + 1) * LANES)
        _select_tile(hd, cols, sc[0][:, cols], sc[1][:, cols], rk_ref, cn_ref, e1_ref, e2_ref)


def _select_tile(hd, cols, s1, s2, rk_ref, cn_ref, e1_ref, e2_ref):
    t = s1.shape[1]
    a = _top_rows(_slabs(s1), N_TOP)
    b = _top_rows(_slabs(s2), N_TOP)
    ninf = jnp.full((1, t), -jnp.inf, F32)
    cand = [a[0] + jnp.concatenate(b[8 * k:8 * k + 8] + [ninf] * (8 * k + 8 - N_TOP), axis=0)
            for k in range((N_TOP + 7) // 8)]
    b8 = jnp.concatenate(b[0:8], axis=0)
    pq = [(p, q) for p in range(1, N_TOP) for q in range(N_TOP // (p + 1))]
    dense = [p for p in range(1, N_TOP) if N_TOP // (p + 1) > 4]
    cand += [a[p] + b8 for p in dense]
    rest = [(p, q) for p, q in pq if p not in dense]
    for k in range(0, len(rest), SLAB):
        grp = rest[k:k + SLAB]
        rows = [a[p] + b[q] for p, q in grp] + [ninf] * (SLAB - len(grp))
        cand.append(jnp.concatenate(rows, axis=0))
    c = _top_rows(cand, N_TOP)
    tau = 0.5 * (c[PEER_TOPK - 1] + c[PEER_TOPK])
    z = functools.reduce(lambda x, y: x + y, [jnp.exp(cr - c[0]) for cr in c[:PEER_TOPK]])
    th = tau - s1
    count = lambda cond: functools.reduce(lambda x, y: x + y, [jnp.where(cq, 1.0, 0.0) for cq in cond])
    cn_ref[hd, :, cols] = count([bq >= th for bq in b])
    rk_ref[hd, :, cols] = count([bq > s2 for bq in b]).astype(BF16)
    e1_ref[hd, :, cols] = jnp.exp(s1 - a[0]) * (0.5 / z)
    e2_ref[hd, :, cols] = jnp.exp(s2 - b[0]).astype(BF16)


def _peer_body(h_ref, gf_ref, wq_ref, sk_ref, u_hbm, vt_hbm, gl_ref, y_ref,
               xt_ref, qt_ref, rk_ref, cn_ref, e1_ref, e2_ref, hid0, hid1, wg0, wg1, acc_ref,
               u_buf, vt_buf, u_sem, vt_sem):
    ne = u_hbm.shape[0]
    hid_refs, wg_refs = (hid0, hid1), (wg0, wg1)

    def u_copy(e, slot):
        return pltpu.make_async_copy(u_hbm.at[e], u_buf.at[slot], u_sem.at[slot])

    def vt_copy(e, slot):
        return pltpu.make_async_copy(vt_hbm.at[e], vt_buf.at[slot], vt_sem.at[slot])

    def start_if_in_range(copy_of, e, slot):
        if isinstance(e, int):
            if e < ne:
                copy_of(e, slot).start()
        else:
            pl.when(e < ne)(lambda: copy_of(e, slot).start())

    def stage_a(rows, slot, par):
        hid_refs[par][rows, :] = jnp.dot(u_buf[slot, rows, :], xt_ref[...], preferred_element_type=F32)

    def stage_b(i, rows, par):
        w = None
        for hd in range(PEER_HEADS):
            cnt = cn_ref[hd, pl.ds(i, 1), :].astype(BF16)
            e1 = e1_ref[hd, pl.ds(i, 1), :].astype(BF16)
            wh = jnp.where(rk_ref[hd] < cnt, e2_ref[hd], jnp.zeros((), BF16)) * e1
            w = wh if w is None else w + wh
        hid = hid_refs[par][rows, :]
        act = hid * (1.0 + lax.erf(hid * math.sqrt(0.5)))
        wg_refs[par][rows, :] = w * act.astype(BF16)

    def stage_c(rows, slot, par, first):
        d = jnp.dot(vt_buf[slot, rows, :], wg_refs[par][...], preferred_element_type=F32)
        if first:
            acc_ref[rows, :] = d
        else:
            acc_ref[rows, :] += d

    def iteration(e, par):
        static = isinstance(e, int)
        do_a = (not static) or e < ne
        do_b = (not static) or 1 <= e <= ne
        do_c = (not static) or e >= 2
        sa, sc = e % N_SLOTS, (e - 2) % N_SLOTS
        if do_a:
            u_copy(e, sa).wait()
        if do_c:
            vt_copy(e - 2, sc).wait()

        def row_group(g, carry):
            nrow = ROW_GROUP * N_KEYS
            rows = pl.ds(pl.multiple_of(g * nrow, nrow), nrow)
            if do_a:
                stage_a(rows, sa, par)
            if do_b:
                for k in range(ROW_GROUP):
                    r = g * ROW_GROUP + k
                    stage_b((e - 1) * ROWS_PER_STEP + r,
                            pl.ds(pl.multiple_of(r * N_KEYS, N_KEYS), N_KEYS), 1 - par)
            if do_c:
                stage_c(rows, sc, par, static and e == 2)
            return carry

        lax.fori_loop(0, ROWS_PER_STEP // ROW_GROUP, row_group, 0)
        if do_a:
            start_if_in_range(u_copy, e + N_SLOTS, sa)
        if do_c:
            start_if_in_range(vt_copy, e - 2 + N_SLOTS, sc)

    for s in range(N_SLOTS):
        u_copy(s, s).start()
        vt_copy(s, s).start()

    hn = _rmsnorm(h_ref[...], gf_ref[...])
    xt_ref[...] = hn.T.astype(BF16)
    qt_ref[...] = jnp.dot(wq_ref[...], xt_ref[...], preferred_element_type=F32)

    def head(hd, carry):
        _peer_select(hd, qt_ref, sk_ref, rk_ref, cn_ref, e1_ref, e2_ref)
        return carry

    lax.fori_loop(0, PEER_HEADS, head, 0)

    def steady(k, carry):
        iteration(2 * k, 0)
        iteration(2 * k + 1, 1)
        return carry

    assert ne % 2 == 0
    for e in range(4):
        iteration(e, e % 2)
    lax.fori_loop(2, ne // 2, steady, 0)
    for e in range(ne, ne + 2):
        iteration(e, e % 2)

    y_ref[...] = _rmsnorm(h_ref[...] + acc_ref[...].T, gl_ref[...])


def _peer(h2, g_ffn, wq_t, sk, u, v_t, g_last):
    n = h2.shape[0]
    t = _row_block(n, 512)
    assert u.shape[0] >= N_SLOTS + 2
    return pl.pallas_call(
        _peer_body,
        grid=(n // t,),
        in_specs=[pl.BlockSpec((t, D_MODEL), lambda i: (i, 0)),
                  _const_spec((1, D_MODEL)),
                  _const_spec(wq_t.shape),
                  _const_spec(sk.shape),
                  pl.BlockSpec(memory_space=pl.ANY),
                  pl.BlockSpec(memory_space=pl.ANY),
                  _const_spec((1, D_MODEL))],
        out_specs=pl.BlockSpec((t, D_MODEL), lambda i: (i, 0)),
        out_shape=jax.ShapeDtypeStruct((n, D_MODEL), F32),
        scratch_shapes=[pltpu.VMEM((D_MODEL, t), BF16),
                        pltpu.VMEM((PEER_HEADS * 2 * PEER_HALF, t), F32),
                        pltpu.VMEM((PEER_HEADS, N_KEYS, t), BF16),
                        pltpu.VMEM((PEER_HEADS, N_KEYS, t), F32),
                        pltpu.VMEM((PEER_HEADS, N_KEYS, t), F32),
                        pltpu.VMEM((PEER_HEADS, N_KEYS, t), BF16),
                        pltpu.VMEM((E_BLOCK, t), F32),
                        pltpu.VMEM((E_BLOCK, t), F32),
                        pltpu.VMEM((E_BLOCK, t), BF16),
                        pltpu.VMEM((E_BLOCK, t), BF16),
                        pltpu.VMEM((D_MODEL, t), F32),
                        pltpu.VMEM((N_SLOTS, E_BLOCK, D_MODEL), BF16),
                        pltpu.VMEM((N_SLOTS, D_MODEL, E_BLOCK), BF16),
                        pltpu.SemaphoreType.DMA((N_SLOTS,)),
                        pltpu.SemaphoreType.DMA((N_SLOTS,))],
        compiler_params=_params("parallel"),
        name="peer",
    )(h2, g_ffn, wq_t, sk, u, v_t, g_last)


def _layer(x3, s0, kc3, vc3, w, bias_tabs):
    b, s, _ = x3.shape
    x2 = x3.reshape(b * s, D_MODEL)
    za, zb, zg = _inproj(x2, w["norm_mix"], w["w_in"])
    za3 = za.reshape(b, s, W_A)
    if kc3 is None:
        oa = _attn_prompt(za3, bias_tabs)
    else:
        oa = _attn_step(za3, kc3, vc3, *bias_tabs)
    ob, s_new = _hgrn(zb.reshape(b, s, W_B), w["lb"], w["hgrn_norm"], s0)
    h2 = _finish(x2, oa.reshape(b * s, D_ATTN), ob.reshape(b * s, D_RNN), zg,
                 w["w_a"], w["w_b"], w["w_out"])
    k_rows = za3[:, :, D_ATTN:2 * D_ATTN]
    v_rows = za3[:, :, 2 * D_ATTN:3 * D_ATTN]
    return h2, k_rows, v_rows, s_new


def kernel(x_prompt, x_sample, cache_attn_k, cache_attn_v, state_hgrn, norm_mix, w_in, rel_bias,
           lb_logits, hgrn_norm, w_branch_a, w_branch_b, w_out, norm_ffn, peer_query, peer_subkeys,
           peer_u, peer_v, norm_final):
    depth = w_in.shape[0]
    bp, sp, _ = x_prompt.shape
    bs, ss, _ = x_sample.shape
    past = cache_attn_k.shape[2]
    keep = min(ATTN_REACH, sp)
    lb_all = jnp.cumsum(jax.nn.softmax(lb_logits.astype(F32), axis=0), axis=0)
    g_last = norm_final.reshape(1, D_MODEL)

    xp, xs = x_prompt, x_sample
    outs = [[] for _ in range(6)]
    for l in range(depth):
        w = dict(
            norm_mix=norm_mix[l].reshape(1, D_MODEL),
            w_in=w_in[l].astype(BF16),
            lb=lb_all[l].reshape(1, D_RNN),
            hgrn_norm=hgrn_norm[l].reshape(1, D_RNN),
            w_a=w_branch_a[l].astype(BF16),
            w_b=w_branch_b[l].astype(BF16),
            w_out=w_out[l].astype(BF16),
        )
        g_ffn = norm_ffn[l].reshape(1, D_MODEL)
        wq_t = peer_query[l].T.astype(BF16)
        sk = peer_subkeys[l].astype(BF16)
        ne = N_EXPERTS // E_BLOCK
        u = peer_u[l].astype(BF16).reshape(ne, E_BLOCK, D_MODEL)
        v_t = peer_v[l].astype(BF16).reshape(ne, E_BLOCK, D_MODEL).transpose(0, 2, 1)
        last = l == depth - 1

        hp, kp, vp, stp = _layer(xp, None, None, None, w, _rel_bias_tables(rel_bias[l]))
        hs, ks, vs, sts = _layer(
            xs, state_hgrn[l],
            cache_attn_k[l].reshape(bs, past, D_ATTN), cache_attn_v[l].reshape(bs, past, D_ATTN),
            w, _step_bias_tables(rel_bias[l], past, ss))
        assert last, "only the final layer fuses the closing rmsnorm"
        xp = _peer(hp, g_ffn, wq_t, sk, u, v_t, g_last).reshape(bp, sp, D_MODEL)
        xs = _peer(hs, g_ffn, wq_t, sk, u, v_t, g_last).reshape(bs, ss, D_MODEL)
        outs[0].append(kp[:, sp - keep:].reshape(bp, keep, N_HEADS_A, HEAD_DIM_A))
        outs[1].append(vp[:, sp - keep:].reshape(bp, keep, N_HEADS_A, HEAD_DIM_A))
        outs[2].append(stp)
        outs[3].append(ks.reshape(bs, ss, N_HEADS_A, HEAD_DIM_A))
        outs[4].append(vs.reshape(bs, ss, N_HEADS_A, HEAD_DIM_A))
        outs[5].append(sts)
    stacked = [jnp.stack(o, axis=0) for o in outs]
    return (xp, xs, *stacked)
```

```python
import functools
import math

import jax
import jax.numpy as jnp
from jax import lax
from jax.experimental import pallas as pl
from jax.experimental.pallas import tpu as pltpu

F32 = jnp.float32
BF16 = jnp.bfloat16

D_MODEL = 1024
CHUNK = 64
N_PAST_CHUNKS = 8
ATTN_REACH = N_PAST_CHUNKS * CHUNK
N_HEADS_A = 8
HEAD_DIM_A = 64
D_ATTN = N_HEADS_A * HEAD_DIM_A
REL_CLIP = 256
ATTN_SCALE = HEAD_DIM_A ** -0.5
N_HEADS_B = 4
DK_B = 128
DV_B = 128
D_RNN = N_HEADS_B * DK_B
PEER_HEADS = 8
N_KEYS = 128
N_EXPERTS = N_KEYS * N_KEYS
PEER_HALF = 128
PEER_TOPK = 16
EPS = 1e-6

LANES = 128
Q_BLOCK = 4 * CHUNK
SUB = 8
MASKED = -1e30
VMEM_LIMIT = 56 * 1024 * 1024


def _params(*sem):
    return pltpu.CompilerParams(dimension_semantics=sem, vmem_limit_bytes=VMEM_LIMIT)


def _const_spec(shape):
    nd = len(shape)
    return pl.BlockSpec(shape, lambda *_: (0,) * nd, pipeline_mode=pl.Buffered(1))


def _row_block(n, cap):
    t = cap
    while n % t:
        t //= 2
    assert t >= LANES, (n, cap)
    return t


def _rmsnorm(x, g):
    return x * lax.rsqrt(jnp.mean(x * x, axis=-1, keepdims=True) + EPS) * g


W_A = 3 * D_ATTN
W_B = 4 * D_RNN
W_G = 2 * D_MODEL


def _inproj_body(x_ref, g_ref, w_ref, za_ref, zb_ref, zg_ref):
    xn = _rmsnorm(x_ref[...], g_ref[...]).astype(BF16)
    za_ref[...] = jnp.dot(xn, w_ref[:, 0:W_A], preferred_element_type=F32)
    zb_ref[...] = jnp.dot(xn, w_ref[:, W_A:W_A + W_B], preferred_element_type=F32)
    zg_ref[...] = jnp.dot(xn, w_ref[:, W_A + W_B:W_A + W_B + W_G], preferred_element_type=F32).astype(BF16)


def _inproj(x2, g, w):
    n = x2.shape[0]
    tm = _row_block(n, 512)
    return pl.pallas_call(
        _inproj_body,
        grid=(n // tm,),
        in_specs=[pl.BlockSpec((tm, D_MODEL), lambda i: (i, 0)),
                  _const_spec((1, D_MODEL)),
                  _const_spec((D_MODEL, W_A + W_B + W_G))],
        out_specs=[pl.BlockSpec((tm, W_A), lambda i: (i, 0)),
                   pl.BlockSpec((tm, W_B), lambda i: (i, 0)),
                   pl.BlockSpec((tm, W_G), lambda i: (i, 0))],
        out_shape=[jax.ShapeDtypeStruct((n, W_A), F32),
                   jax.ShapeDtypeStruct((n, W_B), F32),
                   jax.ShapeDtypeStruct((n, W_G), BF16)],
        compiler_params=_params("parallel"),
        name="inproj",
    )(x2, g, w)


def _attend_pair(q2, parts, bias_of):
    lane = lax.broadcasted_iota(jnp.int32, (1, LANES), 1)
    out = None
    for sub in range(2):
        m = (lane >= HEAD_DIM_A) if sub else (lane < HEAD_DIM_A)
        qm = jnp.where(m, q2 * ATTN_SCALE, 0.0).astype(BF16)
        scores = []
        for pi, (k2, _) in enumerate(parts):
            s = lax.dot_general(qm, k2, (((1,), (1,)), ((), ())), preferred_element_type=F32)
            scores.append(s + bias_of(sub, pi))
        mx = functools.reduce(jnp.maximum, [jnp.max(s, axis=-1, keepdims=True) for s in scores])
        l = None
        o = None
        for s, (_, v2) in zip(scores, parts):
            p = jnp.exp(s - mx)
            ls = jnp.sum(p, axis=-1, keepdims=True)
            os_ = jnp.dot(p.astype(BF16), v2, preferred_element_type=F32)
            l = ls if l is None else l + ls
            o = os_ if o is None else o + os_
        o = o / l
        out = o if out is None else jnp.where(m, o, out)
    return out.astype(BF16)


def _attn_prompt_body(q_ref, k_ref, v_ref, bias_ref, o_ref):
    qi = pl.program_id(1)
    win = Q_BLOCK + ATTN_REACH

    def run(kstart, nk, boff):
        for hp in range(N_HEADS_A // 2):
            ls = slice(hp * LANES, (hp + 1) * LANES)
            k2 = k_ref[0, pl.ds(kstart, nk), ls].astype(BF16)
            v2 = v_ref[0, pl.ds(kstart, nk), ls].astype(BF16)
            o_ref[0, :, ls] = _attend_pair(
                q_ref[0, :, ls], [(k2, v2)],
                lambda sub, pi: bias_ref[2 * hp + sub, :, boff:boff + nk])

    n_short = ATTN_REACH // Q_BLOCK
    for j in range(n_short):
        nk = (j + 1) * Q_BLOCK
        pl.when(qi == j)(functools.partial(run, 0, nk, win - nk))

    @pl.when(qi >= n_short)
    def _():
        run(pl.multiple_of(qi * Q_BLOCK - ATTN_REACH, Q_BLOCK), win, 0)


def _attn_prompt(za3, bias):
    b, s, _ = za3.shape
    assert s % Q_BLOCK == 0 and ATTN_REACH % Q_BLOCK == 0
    return pl.pallas_call(
        _attn_prompt_body,
        grid=(b, s // Q_BLOCK),
        in_specs=[pl.BlockSpec((1, Q_BLOCK, D_ATTN), lambda i, j: (i, j, 0)),
                  pl.BlockSpec((1, s, D_ATTN), lambda i, j: (i, 0, 1)),
                  pl.BlockSpec((1, s, D_ATTN), lambda i, j: (i, 0, 2)),
                  _const_spec(bias.shape)],
        out_specs=pl.BlockSpec((1, Q_BLOCK, D_ATTN), lambda i, j: (i, j, 0)),
        out_shape=jax.ShapeDtypeStruct((b, s, D_ATTN), BF16),
        compiler_params=_params("parallel", "arbitrary"),
        name="attn_prompt",
    )(za3, za3, za3, bias)


def _attn_step_body(q_ref, kn_ref, vn_ref, kc_ref, vc_ref, bc_ref, bn_ref, o_ref):
    for hp in range(N_HEADS_A // 2):
        ls = slice(hp * LANES, (hp + 1) * LANES)
        parts = [(kc_ref[0, :, ls].astype(BF16), vc_ref[0, :, ls].astype(BF16)),
                 (kn_ref[0, :, ls].astype(BF16), vn_ref[0, :, ls].astype(BF16))]
        o_ref[0, :, ls] = _attend_pair(
            q_ref[0, :, ls], parts,
            lambda sub, pi: (bc_ref, bn_ref)[pi][2 * hp + sub])


def _attn_step(za3, kc3, vc3, bias_c, bias_n):
    b, t, _ = za3.shape
    past = kc3.shape[1]
    return pl.pallas_call(
        _attn_step_body,
        grid=(b,),
        in_specs=[pl.BlockSpec((1, t, D_ATTN), lambda i: (i, 0, 0)),
                  pl.BlockSpec((1, t, D_ATTN), lambda i: (i, 0, 1)),
                  pl.BlockSpec((1, t, D_ATTN), lambda i: (i, 0, 2)),
                  pl.BlockSpec((1, past, D_ATTN), lambda i: (i, 0, 0)),
                  pl.BlockSpec((1, past, D_ATTN), lambda i: (i, 0, 0)),
                  _const_spec(bias_c.shape),
                  _const_spec(bias_n.shape)],
        out_specs=pl.BlockSpec((1, t, D_ATTN), lambda i: (i, 0, 0)),
        out_shape=jax.ShapeDtypeStruct((b, t, D_ATTN), BF16),
        compiler_params=_params("parallel"),
        name="attn_step",
    )(za3, za3, za3, kc3, vc3, bias_c, bias_n)


def _toeplitz_bias(rel_bias, tq, nk, past):
    span = tq + nk - 1
    lo = past - (nk - 1) + REL_CLIP
    padl = max(0, -lo)
    padr = max(0, lo + span - 1 - 2 * REL_CLIP)
    ext = jnp.pad(rel_bias.astype(F32), ((0, 0), (padl, padr)), mode="edge")
    g = ext[:, lo + padl:lo + padl + span][:, ::-1]
    flat = jnp.tile(g, (1, tq))[:, tq - 1:tq - 1 + tq * (span - 1)]
    return flat.reshape(-1, tq, span - 1)[:, :, :nk]


def _rel_bias_tables(rel_bias):
    win = Q_BLOCK + ATTN_REACH
    t = jnp.arange(Q_BLOCK)[:, None] // CHUNK
    j = jnp.arange(win)[None, :] // CHUNK
    band = (j >= t) & (j <= t + N_PAST_CHUNKS)
    return jnp.where(band[None], _toeplitz_bias(rel_bias, Q_BLOCK, win, ATTN_REACH), MASKED)


def _step_bias_tables(rel_bias, past, t_new):
    full = _toeplitz_bias(rel_bias, t_new, past + t_new, past)
    return full[:, :, :past], full[:, :, past:]


HGRN_ROWS = CHUNK
HGRN_STREAMS = 4


def _cumsum_rows(x):
    r = x.shape[0]
    assert r & (r - 1) == 0
    hi = x.astype(BF16)
    rest = x - hi.astype(F32)
    mid = rest.astype(BF16)
    lo = (rest - mid.astype(F32)).astype(BF16)
    row = lax.broadcasted_iota(jnp.int32, (r, 3 * r), 0)
    col = lax.broadcasted_iota(jnp.int32, (r, 3 * r), 1)
    tri = jnp.where((col & (r - 1)) <= row, 1.0, 0.0).astype(BF16)
    return jnp.dot(tri, jnp.concatenate([hi, mid, lo], axis=0), preferred_element_type=F32)


def _hgrn_chunk(cum, kk, iv, qr, g, hgh, st_t):
    rows = cum.shape[0]
    qq = qr * jax.nn.sigmoid(qr)
    vb = iv.astype(BF16)
    nsub = rows // SUB

    qe = (qq * jnp.exp(cum)).astype(BF16)
    o = lax.dot_general(qe, st_t.astype(BF16), (((1,), (1,)), ((), ())), preferred_element_type=F32)

    zeros = jnp.zeros((SUB, DK_B), F32)
    refs = [cum[bi * SUB - 1:bi * SUB, :] for bi in range(1, nsub)]
    a_all = qq[SUB:] * jnp.exp(cum[SUB:] - jnp.concatenate(
        [jnp.broadcast_to(r, (SUB, DK_B)) for r in refs], axis=0))
    a_cols, b_cols = [], []
    for bi in range(1, nsub):
        lo = bi * SUB
        a_cols.append(jnp.concatenate(
            [zeros] * bi + [a_all[lo - SUB:lo]] + [zeros] * (nsub - 1 - bi), axis=0))
        b_cols.append(jnp.concatenate(
            [kk[0:lo] * jnp.exp(refs[bi - 1] - cum[0:lo])] + [zeros] * (nsub - bi), axis=0))
    att = lax.dot_general(jnp.concatenate(a_cols, axis=1).astype(BF16),
                          jnp.concatenate(b_cols, axis=1).astype(BF16),
                          (((1,), (1,)), ((), ())), preferred_element_type=F32)
    o = o + jnp.dot(att.astype(BF16), vb, preferred_element_type=F32)

    rmod = lax.broadcasted_iota(jnp.int32, (rows, DK_B), 0) % SUB

    def shift(x, d):
        return pltpu.roll(x.reshape(nsub, SUB, DK_B), d, axis=1).reshape(rows, DK_B)

    o = o + jnp.sum(qq * kk, axis=-1, keepdims=True) * iv
    ckk = cum - jnp.log(kk)
    for d in range(1, SUB):
        p = jnp.where(rmod >= d, qq * jnp.exp(cum - shift(ckk, d)), 0.0)
        o = o + jnp.sum(p, axis=-1, keepdims=True) * shift(iv, d)

    last = cum[rows - 1:rows, :]
    kd = (kk * jnp.exp(last - cum)).astype(BF16)
    st_new = st_t * jnp.exp(last) + jnp.dot(iv.T.astype(BF16), kd, preferred_element_type=F32)

    ob = o * lax.rsqrt(jnp.mean(o * o, axis=-1, keepdims=True) + EPS) * hgh
    ob = ob * (g * jax.nn.sigmoid(g))
    return ob, st_new


def _hgrn_body(*refs, has_s0):
    if has_s0:
        f_ref, i_ref, q_ref, g_ref, lb_ref, hg_ref, s0_ref, o_ref, sn_ref, st_ref = refs
    else:
        f_ref, i_ref, q_ref, g_ref, lb_ref, hg_ref, o_ref, sn_ref, st_ref = refs
    c = pl.program_id(1)
    streams = f_ref.shape[0]

    @pl.when(c == 0)
    def _():
        for b in range(streams):
            for h in range(N_HEADS_B):
                st_ref[b, h] = s0_ref[b, h].T if has_s0 else jnp.zeros((DV_B, DK_B), F32)

    lb = lb_ref[...]
    for b in range(streams):
        fg = lb + (1.0 - lb) * jax.nn.sigmoid(f_ref[b])
        cum = _cumsum_rows(jnp.log(fg))
        kk = 1.0 - fg
        for h in range(N_HEADS_B):
            ls = slice(h * LANES, (h + 1) * LANES)
            ob, st_new = _hgrn_chunk(cum[:, ls], kk[:, ls], i_ref[b, :, ls], q_ref[b, :, ls],
                                     g_ref[b, :, ls], hg_ref[:, ls], st_ref[b, h])
            o_ref[b, :, ls] = ob.astype(BF16)
            st_ref[b, h] = st_new

    @pl.when(c == pl.num_programs(1) - 1)
    def _():
        for b in range(streams):
            for h in range(N_HEADS_B):
                sn_ref[b, h] = st_ref[b, h].T


def _hgrn(zb3, lb, hg, s0):
    b, s, _ = zb3.shape
    rows = min(HGRN_ROWS, s)
    nb = math.gcd(b, HGRN_STREAMS)
    assert s % rows == 0 and rows % SUB == 0
    col = lambda k: pl.BlockSpec((nb, rows, D_RNN), lambda i, j: (i, j, k))
    in_specs = [col(0), col(1), col(2), col(3), _const_spec((1, D_RNN)), _const_spec((1, D_RNN))]
    args = [zb3, zb3, zb3, zb3, lb, hg]
    if s0 is not None:
        in_specs.append(pl.BlockSpec((nb, N_HEADS_B, DK_B, DV_B), lambda i, j: (i, 0, 0, 0)))
        args.append(s0)
    return pl.pallas_call(
        functools.partial(_hgrn_body, has_s0=s0 is not None),
        grid=(b // nb, s // rows),
        in_specs=in_specs,
        out_specs=[pl.BlockSpec((nb, rows, D_RNN), lambda i, j: (i, j, 0)),
                   pl.BlockSpec((nb, N_HEADS_B, DK_B, DV_B), lambda i, j: (i, 0, 0, 0))],
        out_shape=[jax.ShapeDtypeStruct((b, s, D_RNN), BF16),
                   jax.ShapeDtypeStruct((b, N_HEADS_B, DK_B, DV_B), F32)],
        scratch_shapes=[pltpu.VMEM((nb, N_HEADS_B, DV_B, DK_B), F32)],
        compiler_params=_params("parallel", "arbitrary"),
        name="hgrn",
    )(*args)


def _finish_body(x_ref, oa_ref, ob_ref, zg_ref, wa_ref, wb_ref, wo_ref, h_ref):
    ma = jnp.dot(oa_ref[...], wa_ref[...], preferred_element_type=F32)
    mb = jnp.dot(ob_ref[...], wb_ref[...], preferred_element_type=F32)
    merged = (jax.nn.sigmoid(zg_ref[:, 0:D_MODEL].astype(F32)) * ma
              + jax.nn.sigmoid(zg_ref[:, D_MODEL:2 * D_MODEL].astype(F32)) * mb)
    h_ref[...] = x_ref[...] + jnp.dot(merged.astype(BF16), wo_ref[...], preferred_element_type=F32)


def _finish(x2, oa2, ob2, zg, wa, wb, wo):
    n = x2.shape[0]
    tm = _row_block(n, 512)
    row = lambda w: pl.BlockSpec((tm, w), lambda i: (i, 0))
    return pl.pallas_call(
        _finish_body,
        grid=(n // tm,),
        in_specs=[row(D_MODEL), row(D_ATTN), row(D_RNN), row(W_G),
                  _const_spec(wa.shape), _const_spec(wb.shape), _const_spec(wo.shape)],
        out_specs=row(D_MODEL),
        out_shape=jax.ShapeDtypeStruct((n, D_MODEL), F32),
        compiler_params=_params("parallel"),
        name="finish",
    )(x2, oa2, ob2, zg, wa, wb, wo)


N_TOP = PEER_TOPK + 1
E_BLOCK = D_MODEL
ROWS_PER_STEP = E_BLOCK // N_KEYS
N_SLOTS = 2
ROW_GROUP = 8


SLAB = 8


def _sort_network(n):
    pairs, p = [], 1
    while p < n:
        k = p
        while k >= 1:
            for j in range(k % p, n - k, 2 * k):
                for i in range(min(k, n - j - k)):
                    if (i + j) // (2 * p) == (i + j + k) // (2 * p):
                        pairs.append((i + j, i + j + k))
            k //= 2
        p *= 2
    return pairs


def _top_rows(slabs, n):
    v = list(slabs)
    width = 1
    while width < len(v):
        width *= 2
    for i, j in _sort_network(width):
        if j < len(v):
            v[i], v[j] = jnp.maximum(v[i], v[j]), jnp.minimum(v[i], v[j])
    rows = []
    for r in range(n):
        mx = jnp.max(v[0], axis=0, keepdims=True)
        rows.append(mx)
        if r + 1 < n:
            hit = v[0] == mx
            keep = min(len(v), n - 1 - r)
            v = [jnp.where(hit, v[k + 1] if k + 1 < len(v) else -jnp.inf, v[k]) for k in range(keep)]
    return rows


def _slabs(x):
    return [x[k * SLAB:(k + 1) * SLAB, :] for k in range(x.shape[0] // SLAB)]


def _peer_select(hd, qt_ref, sk_ref, rk_ref, cn_ref, e1_ref, e2_ref):
    t = qt_ref.shape[1]
    sc = []
    for p in range(2):
        qhp = qt_ref[pl.ds(pl.multiple_of((hd * 2 + p) * PEER_HALF, PEER_HALF), PEER_HALF), :]
        sc.append(jnp.dot(sk_ref[p, hd], qhp.astype(BF16), preferred_element_type=F32))
    for c in range(t // LANES):
        cols = slice(c * LANES, (c + 1) * LANES)
        _select_tile(hd, cols, sc[0][:, cols], sc[1][:, cols], rk_ref, cn_ref, e1_ref, e2_ref)


def _select_tile(hd, cols, s1, s2, rk_ref, cn_ref, e1_ref, e2_ref):
    t = s1.shape[1]
    a = _top_rows(_slabs(s1), N_TOP)
    b = _top_rows(_slabs(s2), N_TOP)
    ninf = jnp.full((1, t), -jnp.inf, F32)
    cand = [a[0] + jnp.concatenate(b[8 * k:8 * k + 8] + [ninf] * (8 * k + 8 - N_TOP), axis=0)
            for k in range((N_TOP + 7) // 8)]
    b8 = jnp.concatenate(b[0:8], axis=0)
    pq = [(p, q) for p in range(1, N_TOP) for q in range(N_TOP // (p + 1))]
    dense = [p for p in range(1, N_TOP) if N_TOP // (p + 1) > 4]
    cand += [a[p] + b8 for p in dense]
    rest = [(p, q) for p, q in pq if p not in dense]
    for k in range(0, len(rest), SLAB):
        grp = rest[k:k + SLAB]
        rows = [a[p] + b[q] for p, q in grp] + [ninf] * (SLAB - len(grp))
        cand.append(jnp.concatenate(rows, axis=0))
    c = _top_rows(cand, N_TOP)
    tau = 0.5 * (c[PEER_TOPK - 1] + c[PEER_TOPK])
    z = functools.reduce(lambda x, y: x + y, [jnp.exp(cr - c[0]) for cr in c[:PEER_TOPK]])
    th = tau - s1
    def count(conds):
        n = jnp.zeros(s1.shape, F32)
        for q, cq in enumerate(conds):
            n = jnp.where(cq, float(q + 1), n)
        return n

    cn_ref[hd, :, cols] = count([bq >= th for bq in b])
    rk_ref[hd, :, cols] = count([bq > s2 for bq in b]).astype(BF16)
    e1_ref[hd, :, cols] = jnp.exp(s1 - a[0]) * (0.5 / z)
    e2_ref[hd, :, cols] = jnp.exp(s2 - b[0]).astype(BF16)


def _peer_body(h_ref, gf_ref, wq_ref, sk_ref, u_hbm, vt_hbm, gl_ref, y_ref,
               xt_ref, qt_ref, rk_ref, cn_ref, e1_ref, e2_ref, hid0, hid1, wg0, wg1, acc_ref,
               u_buf, vt_buf, u_sem, vt_sem):
    ne = u_hbm.shape[0]
    hid_refs, wg_refs = (hid0, hid1), (wg0, wg1)

    def u_copy(e, slot):
        return pltpu.make_async_copy(u_hbm.at[e], u_buf.at[slot], u_sem.at[slot])

    def vt_copy(e, slot):
        return pltpu.make_async_copy(vt_hbm.at[e], vt_buf.at[slot], vt_sem.at[slot])

    def start_if_in_range(copy_of, e, slot):
        if isinstance(e, int):
            if e < ne:
                copy_of(e, slot).start()
        else:
            pl.when(e < ne)(lambda: copy_of(e, slot).start())

    def stage_a(rows, slot, par):
        hid_refs[par][rows, :] = jnp.dot(u_buf[slot, rows, :], xt_ref[...], preferred_element_type=F32)

    def stage_b(i, rows, par):
        w = None
        for hd in range(PEER_HEADS):
            cnt = cn_ref[hd, pl.ds(i, 1), :].astype(BF16)
            e1 = e1_ref[hd, pl.ds(i, 1), :].astype(BF16)
            wh = jnp.where(rk_ref[hd] < cnt, e2_ref[hd], jnp.zeros((), BF16)) * e1
            w = wh if w is None else w + wh
        hid = hid_refs[par][rows, :]
        act = hid * (1.0 + lax.erf(hid * math.sqrt(0.5)))
        wg_refs[par][rows, :] = w * act.astype(BF16)

    def stage_c(rows, slot, par, first):
        d = jnp.dot(vt_buf[slot, rows, :], wg_refs[par][...], preferred_element_type=F32)
        if first:
            acc_ref[rows, :] = d
        else:
            acc_ref[rows, :] += d

    def iteration(e, par):
        static = isinstance(e, int)
        do_a = (not static) or e < ne
        do_b = (not static) or 1 <= e <= ne
        do_c = (not static) or e >= 2
        sa, sc = e % N_SLOTS, (e - 2) % N_SLOTS
        if do_a:
            u_copy(e, sa).wait()
        if do_c:
            vt_copy(e - 2, sc).wait()

        def row_group(g, carry):
            nrow = ROW_GROUP * N_KEYS
            rows = pl.ds(pl.multiple_of(g * nrow, nrow), nrow)
            if do_a:
                stage_a(rows, sa, par)
            if do_b:
                for k in range(ROW_GROUP):
                    r = g * ROW_GROUP + k
                    stage_b((e - 1) * ROWS_PER_STEP + r,
                            pl.ds(pl.multiple_of(r * N_KEYS, N_KEYS), N_KEYS), 1 - par)
            if do_c:
                stage_c(rows, sc, par, static and e == 2)
            return carry

        lax.fori_loop(0, ROWS_PER_STEP // ROW_GROUP, row_group, 0)
        if do_a:
            start_if_in_range(u_copy, e + N_SLOTS, sa)
        if do_c:
            start_if_in_range(vt_copy, e - 2 + N_SLOTS, sc)

    for s in range(N_SLOTS):
        u_copy(s, s).start()
        vt_copy(s, s).start()

    hn = _rmsnorm(h_ref[...], gf_ref[...])
    xt_ref[...] = hn.T.astype(BF16)
    qt_ref[...] = jnp.dot(wq_ref[...], xt_ref[...], preferred_element_type=F32)

    def head(hd, carry):
        _peer_select(hd, qt_ref, sk_ref, rk_ref, cn_ref, e1_ref, e2_ref)
        return carry

    lax.fori_loop(0, PEER_HEADS, head, 0)

    def steady(k, carry):
        iteration(2 * k, 0)
        iteration(2 * k + 1, 1)
        return carry

    assert ne % 2 == 0
    for e in range(4):
        iteration(e, e % 2)
    lax.fori_loop(2, ne // 2, steady, 0)
    for e in range(ne, ne + 2):
        iteration(e, e % 2)

    y_ref[...] = _rmsnorm(h_ref[...] + acc_ref[...].T, gl_ref[...])


def _peer(h2, g_ffn, wq_t, sk, u, v_t, g_last):
    n = h2.shape[0]
    t = _row_block(n, 512)
    assert u.shape[0] >= N_SLOTS + 2
    return pl.pallas_call(
        _peer_body,
        grid=(n // t,),
        in_specs=[pl.BlockSpec((t, D_MODEL), lambda i: (i, 0)),
                  _const_spec((1, D_MODEL)),
                  _const_spec(wq_t.shape),
                  _const_spec(sk.shape),
                  pl.BlockSpec(memory_space=pl.ANY),
                  pl.BlockSpec(memory_space=pl.ANY),
                  _const_spec((1, D_MODEL))],
        out_specs=pl.BlockSpec((t, D_MODEL), lambda i: (i, 0)),
        out_shape=jax.ShapeDtypeStruct((n, D_MODEL), F32),
        scratch_shapes=[pltpu.VMEM((D_MODEL, t), BF16),
                        pltpu.VMEM((PEER_HEADS * 2 * PEER_HALF, t), F32),
                        pltpu.VMEM((PEER_HEADS, N_KEYS, t), BF16),
                        pltpu.VMEM((PEER_HEADS, N_KEYS, t), F32),
                        pltpu.VMEM((PEER_HEADS, N_KEYS, t), F32),
                        pltpu.VMEM((PEER_HEADS, N_KEYS, t), BF16),
                        pltpu.VMEM((E_BLOCK, t), F32),
                        pltpu.VMEM((E_BLOCK, t), F32),
                        pltpu.VMEM((E_BLOCK, t), BF16),
                        pltpu.VMEM((E_BLOCK, t), BF16),
                        pltpu.VMEM((D_MODEL, t), F32),
                        pltpu.VMEM((N_SLOTS, E_BLOCK, D_MODEL), BF16),
                        pltpu.VMEM((N_SLOTS, D_MODEL, E_BLOCK), BF16),
                        pltpu.SemaphoreType.DMA((N_SLOTS,)),
                        pltpu.SemaphoreType.DMA((N_SLOTS,))],
        compiler_params=_params("parallel"),
        name="peer",
    )(h2, g_ffn, wq_t, sk, u, v_t, g_last)


def _layer(x3, s0, kc3, vc3, w, bias_tabs):
    b, s, _ = x3.shape
    x2 = x3.reshape(b * s, D_MODEL)
    za, zb, zg = _inproj(x2, w["norm_mix"], w["w_in"])
    za3 = za.reshape(b, s, W_A)
    if kc3 is None:
        oa = _attn_prompt(za3, bias_tabs)
    else:
        oa = _attn_step(za3, kc3, vc3, *bias_tabs)
    ob, s_new = _hgrn(zb.reshape(b, s, W_B), w["lb"], w["hgrn_norm"], s0)
    h2 = _finish(x2, oa.reshape(b * s, D_ATTN), ob.reshape(b * s, D_RNN), zg,
                 w["w_a"], w["w_b"], w["w_out"])
    k_rows = za3[:, :, D_ATTN:2 * D_ATTN]
    v_rows = za3[:, :, 2 * D_ATTN:3 * D_ATTN]
    return h2, k_rows, v_rows, s_new


def kernel(x_prompt, x_sample, cache_attn_k, cache_attn_v, state_hgrn, norm_mix, w_in, rel_bias,
           lb_logits, hgrn_norm, w_branch_a, w_branch_b, w_out, norm_ffn, peer_query, peer_subkeys,
           peer_u, peer_v, norm_final):
    depth = w_in.shape[0]
    bp, sp, _ = x_prompt.shape
    bs, ss, _ = x_sample.shape
    past = cache_attn_k.shape[2]
    keep = min(ATTN_REACH, sp)
    lb_all = jnp.cumsum(jax.nn.softmax(lb_logits.astype(F32), axis=0), axis=0)
    g_last = norm_final.reshape(1, D_MODEL)

    xp, xs = x_prompt, x_sample
    outs = [[] for _ in range(6)]
    for l in range(depth):
        w = dict(
            norm_mix=norm_mix[l].reshape(1, D_MODEL),
            w_in=w_in[l].astype(BF16),
            lb=lb_all[l].reshape(1, D_RNN),
            hgrn_norm=hgrn_norm[l].reshape(1, D_RNN),
            w_a=w_branch_a[l].astype(BF16),
            w_b=w_branch_b[l].astype(BF16),
            w_out=w_out[l].astype(BF16),
        )
        g_ffn = norm_ffn[l].reshape(1, D_MODEL)
        wq_t = peer_query[l].T.astype(BF16)
        sk = peer_subkeys[l].astype(BF16)
        ne = N_EXPERTS // E_BLOCK
        u = peer_u[l].astype(BF16).reshape(ne, E_BLOCK, D_MODEL)
        v_t = peer_v[l].astype(BF16).reshape(ne, E_BLOCK, D_MODEL).transpose(0, 2, 1)
        last = l == depth - 1

        hp, kp, vp, stp = _layer(xp, None, None, None, w, _rel_bias_tables(rel_bias[l]))
        hs, ks, vs, sts = _layer(
            xs, state_hgrn[l],
            cache_attn_k[l].reshape(bs, past, D_ATTN), cache_attn_v[l].reshape(bs, past, D_ATTN),
            w, _step_bias_tables(rel_bias[l], past, ss))
        assert last, "only the final layer fuses the closing rmsnorm"
        xp = _peer(hp, g_ffn, wq_t, sk, u, v_t, g_last).reshape(bp, sp, D_MODEL)
        xs = _peer(hs, g_ffn, wq_t, sk, u, v_t, g_last).reshape(bs, ss, D_MODEL)
        outs[0].append(kp[:, sp - keep:].reshape(bp, keep, N_HEADS_A, HEAD_DIM_A))
        outs[1].append(vp[:, sp - keep:].reshape(bp, keep, N_HEADS_A, HEAD_DIM_A))
        outs[2].append(stp)
        outs[3].append(ks.reshape(bs, ss, N_HEADS_A, HEAD_DIM_A))
        outs[4].append(vs.reshape(bs, ss, N_HEADS_A, HEAD_DIM_A))
        outs[5].append(sts)
    stacked = [jnp.stack(o, axis=0) for o in outs]
    return (xp, xs, *stacked)
```

```python
import functools
import math

import jax
import jax.numpy as jnp
from jax import lax
from jax.experimental import pallas as pl
from jax.experimental.pallas import tpu as pltpu

F32 = jnp.float32
BF16 = jnp.bfloat16

D_MODEL = 1024
CHUNK = 64
N_PAST_CHUNKS = 8
ATTN_REACH = N_PAST_CHUNKS * CHUNK
N_HEADS_A = 8
HEAD_DIM_A = 64
D_ATTN = N_HEADS_A * HEAD_DIM_A
REL_CLIP = 256
ATTN_SCALE = HEAD_DIM_A ** -0.5
N_HEADS_B = 4
DK_B = 128
DV_B = 128
D_RNN = N_HEADS_B * DK_B
PEER_HEADS = 8
N_KEYS = 128
N_EXPERTS = N_KEYS * N_KEYS
PEER_HALF = 128
PEER_TOPK = 16
EPS = 1e-6

LANES = 128
Q_BLOCK = 4 * CHUNK
SUB = 8
MASKED = -1e30
VMEM_LIMIT = 56 * 1024 * 1024


def _params(*sem):
    return pltpu.CompilerParams(dimension_semantics=sem, vmem_limit_bytes=VMEM_LIMIT)


def _const_spec(shape):
    nd = len(shape)
    return pl.BlockSpec(shape, lambda *_: (0,) * nd, pipeline_mode=pl.Buffered(1))


def _row_block(n, cap):
    t = cap
    while n % t:
        t //= 2
    assert t >= LANES, (n, cap)
    return t


def _rmsnorm(x, g):
    return x * lax.rsqrt(jnp.mean(x * x, axis=-1, keepdims=True) + EPS) * g


W_A = 3 * D_ATTN
W_B = 4 * D_RNN
W_G = 2 * D_MODEL


def _inproj_body(x_ref, g_ref, w_ref, za_ref, zb_ref, zg_ref):
    xn = _rmsnorm(x_ref[...], g_ref[...]).astype(BF16)
    za_ref[...] = jnp.dot(xn, w_ref[:, 0:W_A], preferred_element_type=F32)
    zb_ref[...] = jnp.dot(xn, w_ref[:, W_A:W_A + W_B], preferred_element_type=F32)
    zg_ref[...] = jnp.dot(xn, w_ref[:, W_A + W_B:W_A + W_B + W_G], preferred_element_type=F32).astype(BF16)


def _inproj(x2, g, w):
    n = x2.shape[0]
    tm = _row_block(n, 512)
    return pl.pallas_call(
        _inproj_body,
        grid=(n // tm,),
        in_specs=[pl.BlockSpec((tm, D_MODEL), lambda i: (i, 0)),
                  _const_spec((1, D_MODEL)),
                  _const_spec((D_MODEL, W_A + W_B + W_G))],
        out_specs=[pl.BlockSpec((tm, W_A), lambda i: (i, 0)),
                   pl.BlockSpec((tm, W_B), lambda i: (i, 0)),
                   pl.BlockSpec((tm, W_G), lambda i: (i, 0))],
        out_shape=[jax.ShapeDtypeStruct((n, W_A), F32),
                   jax.ShapeDtypeStruct((n, W_B), F32),
                   jax.ShapeDtypeStruct((n, W_G), BF16)],
        compiler_params=_params("parallel"),
        name="inproj",
    )(x2, g, w)


def _attend_pair(q2, parts, bias_of):
    tq = q2.shape[0]
    first = lax.broadcasted_iota(jnp.int32, (1, LANES), 1) < HEAD_DIM_A
    qs = q2 * ATTN_SCALE
    qm = jnp.concatenate([jnp.where(first, qs, 0.0), jnp.where(first, 0.0, qs)], axis=0).astype(BF16)
    scores = []
    for pi, (k2, _) in enumerate(parts):
        s = lax.dot_general(qm, k2, (((1,), (1,)), ((), ())), preferred_element_type=F32)
        scores.append(s + jnp.concatenate([bias_of(0, pi), bias_of(1, pi)], axis=0))
    mx = functools.reduce(jnp.maximum, [jnp.max(s, axis=-1, keepdims=True) for s in scores])
    l = None
    o = None
    for s, (_, v2) in zip(scores, parts):
        p = jnp.exp(s - mx)
        ls = jnp.sum(p, axis=-1, keepdims=True)
        os_ = jnp.dot(p.astype(BF16), v2, preferred_element_type=F32)
        l = ls if l is None else l + ls
        o = os_ if o is None else o + os_
    o = o / l
    return jnp.where(first, o[:tq], o[tq:]).astype(BF16)


def _attn_prompt_body(q_ref, k_ref, v_ref, bias_ref, o_ref):
    qi = pl.program_id(1)
    win = Q_BLOCK + ATTN_REACH

    def run(kstart, nk, boff):
        for hp in range(N_HEADS_A // 2):
            ls = slice(hp * LANES, (hp + 1) * LANES)
            k2 = k_ref[0, pl.ds(kstart, nk), ls].astype(BF16)
            v2 = v_ref[0, pl.ds(kstart, nk), ls].astype(BF16)
            o_ref[0, :, ls] = _attend_pair(
                q_ref[0, :, ls], [(k2, v2)],
                lambda sub, pi: bias_ref[2 * hp + sub, :, boff:boff + nk])

    n_short = ATTN_REACH // Q_BLOCK
    for j in range(n_short):
        nk = (j + 1) * Q_BLOCK
        pl.when(qi == j)(functools.partial(run, 0, nk, win - nk))

    @pl.when(qi >= n_short)
    def _():
        run(pl.multiple_of(qi * Q_BLOCK - ATTN_REACH, Q_BLOCK), win, 0)


def _attn_prompt(za3, bias):
    b, s, _ = za3.shape
    assert s % Q_BLOCK == 0 and ATTN_REACH % Q_BLOCK == 0
    return pl.pallas_call(
        _attn_prompt_body,
        grid=(b, s // Q_BLOCK),
        in_specs=[pl.BlockSpec((1, Q_BLOCK, D_ATTN), lambda i, j: (i, j, 0)),
                  pl.BlockSpec((1, s, D_ATTN), lambda i, j: (i, 0, 1)),
                  pl.BlockSpec((1, s, D_ATTN), lambda i, j: (i, 0, 2)),
                  _const_spec(bias.shape)],
        out_specs=pl.BlockSpec((1, Q_BLOCK, D_ATTN), lambda i, j: (i, j, 0)),
        out_shape=jax.ShapeDtypeStruct((b, s, D_ATTN), BF16),
        compiler_params=_params("parallel", "arbitrary"),
        name="attn_prompt",
    )(za3, za3, za3, bias)


def _attn_step_body(q_ref, kn_ref, vn_ref, kc_ref, vc_ref, bc_ref, bn_ref, o_ref):
    for hp in range(N_HEADS_A // 2):
        ls = slice(hp * LANES, (hp + 1) * LANES)
        parts = [(kc_ref[0, :, ls].astype(BF16), vc_ref[0, :, ls].astype(BF16)),
                 (kn_ref[0, :, ls].astype(BF16), vn_ref[0, :, ls].astype(BF16))]
        o_ref[0, :, ls] = _attend_pair(
            q_ref[0, :, ls], parts,
            lambda sub, pi: (bc_ref, bn_ref)[pi][2 * hp + sub])


def _attn_step(za3, kc3, vc3, bias_c, bias_n):
    b, t, _ = za3.shape
    past = kc3.shape[1]
    return pl.pallas_call(
        _attn_step_body,
        grid=(b,),
        in_specs=[pl.BlockSpec((1, t, D_ATTN), lambda i: (i, 0, 0)),
                  pl.BlockSpec((1, t, D_ATTN), lambda i: (i, 0, 1)),
                  pl.BlockSpec((1, t, D_ATTN), lambda i: (i, 0, 2)),
                  pl.BlockSpec((1, past, D_ATTN), lambda i: (i, 0, 0)),
                  pl.BlockSpec((1, past, D_ATTN), lambda i: (i, 0, 0)),
                  _const_spec(bias_c.shape),
                  _const_spec(bias_n.shape)],
        out_specs=pl.BlockSpec((1, t, D_ATTN), lambda i: (i, 0, 0)),
        out_shape=jax.ShapeDtypeStruct((b, t, D_ATTN), BF16),
        compiler_params=_params("parallel"),
        name="attn_step",
    )(za3, za3, za3, kc3, vc3, bias_c, bias_n)


def _toeplitz_bias(rel_bias, tq, nk, past):
    span = tq + nk - 1
    lo = past - (nk - 1) + REL_CLIP
    padl = max(0, -lo)
    padr = max(0, lo + span - 1 - 2 * REL_CLIP)
    ext = jnp.pad(rel_bias.astype(F32), ((0, 0), (padl, padr)), mode="edge")
    g = ext[:, lo + padl:lo + padl + span][:, ::-1]
    flat = jnp.tile(g, (1, tq))[:, tq - 1:tq - 1 + tq * (span - 1)]
    return flat.reshape(-1, tq, span - 1)[:, :, :nk]


def _rel_bias_tables(rel_bias):
    win = Q_BLOCK + ATTN_REACH
    t = jnp.arange(Q_BLOCK)[:, None] // CHUNK
    j = jnp.arange(win)[None, :] // CHUNK
    band = (j >= t) & (j <= t + N_PAST_CHUNKS)
    return jnp.where(band[None], _toeplitz_bias(rel_bias, Q_BLOCK, win, ATTN_REACH), MASKED)


def _step_bias_tables(rel_bias, past, t_new):
    full = _toeplitz_bias(rel_bias, t_new, past + t_new, past)
    return full[:, :, :past], full[:, :, past:]


HGRN_ROWS = CHUNK
HGRN_STREAMS = 4


def _cumsum_rows(x):
    r = x.shape[0]
    assert r & (r - 1) == 0
    hi = x.astype(BF16)
    rest = x - hi.astype(F32)
    mid = rest.astype(BF16)
    lo = (rest - mid.astype(F32)).astype(BF16)
    row = lax.broadcasted_iota(jnp.int32, (r, 3 * r), 0)
    col = lax.broadcasted_iota(jnp.int32, (r, 3 * r), 1)
    tri = jnp.where((col & (r - 1)) <= row, 1.0, 0.0).astype(BF16)
    return jnp.dot(tri, jnp.concatenate([hi, mid, lo], axis=0), preferred_element_type=F32)


def _hgrn_chunk(cum, kk, iv, qr, g, hgh, st_t):
    rows = cum.shape[0]
    qq = qr * jax.nn.sigmoid(qr)
    vb = iv.astype(BF16)
    nsub = rows // SUB

    qe = (qq * jnp.exp(cum)).astype(BF16)
    o = lax.dot_general(qe, st_t.astype(BF16), (((1,), (1,)), ((), ())), preferred_element_type=F32)

    zeros = jnp.zeros((SUB, DK_B), F32)
    refs = [cum[bi * SUB - 1:bi * SUB, :] for bi in range(1, nsub)]
    a_all = qq[SUB:] * jnp.exp(cum[SUB:] - jnp.concatenate(
        [jnp.broadcast_to(r, (SUB, DK_B)) for r in refs], axis=0))
    a_cols, b_cols = [], []
    for bi in range(1, nsub):
        lo = bi * SUB
        a_cols.append(jnp.concatenate(
            [zeros] * bi + [a_all[lo - SUB:lo]] + [zeros] * (nsub - 1 - bi), axis=0))
        b_cols.append(jnp.concatenate(
            [kk[0:lo] * jnp.exp(refs[bi - 1] - cum[0:lo])] + [zeros] * (nsub - bi), axis=0))
    att = lax.dot_general(jnp.concatenate(a_cols, axis=1).astype(BF16),
                          jnp.concatenate(b_cols, axis=1).astype(BF16),
                          (((1,), (1,)), ((), ())), preferred_element_type=F32)
    o = o + jnp.dot(att.astype(BF16), vb, preferred_element_type=F32)

    rmod = lax.broadcasted_iota(jnp.int32, (rows, DK_B), 0) % SUB

    def shift(x, d):
        return pltpu.roll(x.reshape(nsub, SUB, DK_B), d, axis=1).reshape(rows, DK_B)

    o = o + jnp.sum(qq * kk, axis=-1, keepdims=True) * iv
    ckk = cum - jnp.log(kk)
    for d in range(1, SUB):
        p = jnp.where(rmod >= d, qq * jnp.exp(cum - shift(ckk, d)), 0.0)
        o = o + jnp.sum(p, axis=-1, keepdims=True) * shift(iv, d)

    last = cum[rows - 1:rows, :]
    kd = (kk * jnp.exp(last - cum)).astype(BF16)
    st_new = st_t * jnp.exp(last) + jnp.dot(iv.T.astype(BF16), kd, preferred_element_type=F32)

    ob = o * lax.rsqrt(jnp.mean(o * o, axis=-1, keepdims=True) + EPS) * hgh
    ob = ob * (g * jax.nn.sigmoid(g))
    return ob, st_new


def _hgrn_body(*refs, has_s0):
    if has_s0:
        f_ref, i_ref, q_ref, g_ref, lb_ref, hg_ref, s0_ref, o_ref, sn_ref, st_ref = refs
    else:
        f_ref, i_ref, q_ref, g_ref, lb_ref, hg_ref, o_ref, sn_ref, st_ref = refs
    c = pl.program_id(1)
    streams = f_ref.shape[0]

    @pl.when(c == 0)
    def _():
        for b in range(streams):
            for h in range(N_HEADS_B):
                st_ref[b, h] = s0_ref[b, h].T if has_s0 else jnp.zeros((DV_B, DK_B), F32)

    lb = lb_ref[...]
    for b in range(streams):
        fg = lb + (1.0 - lb) * jax.nn.sigmoid(f_ref[b])
        cum = _cumsum_rows(jnp.log(fg))
        kk = 1.0 - fg
        for h in range(N_HEADS_B):
            ls = slice(h * LANES, (h + 1) * LANES)
            ob, st_new = _hgrn_chunk(cum[:, ls], kk[:, ls], i_ref[b, :, ls], q_ref[b, :, ls],
                                     g_ref[b, :, ls], hg_ref[:, ls], st_ref[b, h])
            o_ref[b, :, ls] = ob.astype(BF16)
            st_ref[b, h] = st_new

    @pl.when(c == pl.num_programs(1) - 1)
    def _():
        for b in range(streams):
            for h in range(N_HEADS_B):
                sn_ref[b, h] = st_ref[b, h].T


def _hgrn(zb3, lb, hg, s0):
    b, s, _ = zb3.shape
    rows = min(HGRN_ROWS, s)
    nb = math.gcd(b, HGRN_STREAMS)
    assert s % rows == 0 and rows % SUB == 0
    col = lambda k: pl.BlockSpec((nb, rows, D_RNN), lambda i, j: (i, j, k))
    in_specs = [col(0), col(1), col(2), col(3), _const_spec((1, D_RNN)), _const_spec((1, D_RNN))]
    args = [zb3, zb3, zb3, zb3, lb, hg]
    if s0 is not None:
        in_specs.append(pl.BlockSpec((nb, N_HEADS_B, DK_B, DV_B), lambda i, j: (i, 0, 0, 0)))
        args.append(s0)
    return pl.pallas_call(
        functools.partial(_hgrn_body, has_s0=s0 is not None),
        grid=(b // nb, s // rows),
        in_specs=in_specs,
        out_specs=[pl.BlockSpec((nb, rows, D_RNN), lambda i, j: (i, j, 0)),
                   pl.BlockSpec((nb, N_HEADS_B, DK_B, DV_B), lambda i, j: (i, 0, 0, 0))],
        out_shape=[jax.ShapeDtypeStruct((b, s, D_RNN), BF16),
                   jax.ShapeDtypeStruct((b, N_HEADS_B, DK_B, DV_B), F32)],
        scratch_shapes=[pltpu.VMEM((nb, N_HEADS_B, DV_B, DK_B), F32)],
        compiler_params=_params("parallel", "arbitrary"),
        name="hgrn",
    )(*args)


def _finish_body(x_ref, oa_ref, ob_ref, zg_ref, wa_ref, wb_ref, wo_ref, h_ref):
    ma = jnp.dot(oa_ref[...], wa_ref[...], preferred_element_type=F32)
    mb = jnp.dot(ob_ref[...], wb_ref[...], preferred_element_type=F32)
    merged = (jax.nn.sigmoid(zg_ref[:, 0:D_MODEL].astype(F32)) * ma
              + jax.nn.sigmoid(zg_ref[:, D_MODEL:2 * D_MODEL].astype(F32)) * mb)
    h_ref[...] = x_ref[...] + jnp.dot(merged.astype(BF16), wo_ref[...], preferred_element_type=F32)


def _finish(x2, oa2, ob2, zg, wa, wb, wo):
    n = x2.shape[0]
    tm = _row_block(n, 1024)
    row = lambda w: pl.BlockSpec((tm, w), lambda i: (i, 0))
    return pl.pallas_call(
        _finish_body,
        grid=(n // tm,),
        in_specs=[row(D_MODEL), row(D_ATTN), row(D_RNN), row(W_G),
                  _const_spec(wa.shape), _const_spec(wb.shape), _const_spec(wo.shape)],
        out_specs=row(D_MODEL),
        out_shape=jax.ShapeDtypeStruct((n, D_MODEL), F32),
        compiler_params=_params("parallel"),
        name="finish",
    )(x2, oa2, ob2, zg, wa, wb, wo)


N_TOP = PEER_TOPK + 1
E_BLOCK = D_MODEL
ROWS_PER_STEP = E_BLOCK // N_KEYS
N_SLOTS = 2
ROW_GROUP = 8


SLAB = 8


def _sort_network(n):
    pairs, p = [], 1
    while p < n:
        k = p
        while k >= 1:
            for j in range(k % p, n - k, 2 * k):
                for i in range(min(k, n - j - k)):
                    if (i + j) // (2 * p) == (i + j + k) // (2 * p):
                        pairs.append((i + j, i + j + k))
            k //= 2
        p *= 2
    return pairs


def _top_rows(slabs, n):
    v = list(slabs)
    width = 1
    while width < len(v):
        width *= 2
    for i, j in _sort_network(width):
        if j < len(v):
            v[i], v[j] = jnp.maximum(v[i], v[j]), jnp.minimum(v[i], v[j])
    rows = []
    for r in range(n):
        mx = jnp.max(v[0], axis=0, keepdims=True)
        rows.append(mx)
        if r + 1 < n:
            hit = v[0] == mx
            keep = min(len(v), n - 1 - r)
            v = [jnp.where(hit, v[k + 1] if k + 1 < len(v) else -jnp.inf, v[k]) for k in range(keep)]
    return rows


def _slabs(x):
    return [x[k * SLAB:(k + 1) * SLAB, :] for k in range(x.shape[0] // SLAB)]


def _peer_select(hd, qt_ref, sk_ref, rk_ref, cn_ref, e1_ref, e2_ref):
    t = qt_ref.shape[1]
    sc = []
    for p in range(2):
        qhp = qt_ref[pl.ds(pl.multiple_of((hd * 2 + p) * PEER_HALF, PEER_HALF), PEER_HALF), :]
        sc.append(jnp.dot(sk_ref[p, hd], qhp.astype(BF16), preferred_element_type=F32))
    for c in range(t // LANES):
        cols = slice(c * LANES, (c + 1) * LANES)
        _select_tile(hd, cols, sc[0][:, cols], sc[1][:, cols], rk_ref, cn_ref, e1_ref, e2_ref)


def _select_tile(hd, cols, s1, s2, rk_ref, cn_ref, e1_ref, e2_ref):
    t = s1.shape[1]
    a = _top_rows(_slabs(s1), N_TOP)
    b = _top_rows(_slabs(s2), N_TOP)
    ninf = jnp.full((1, t), -jnp.inf, F32)
    cand = [a[0] + jnp.concatenate(b[8 * k:8 * k + 8] + [ninf] * (8 * k + 8 - N_TOP), axis=0)
            for k in range((N_TOP + 7) // 8)]
    b8 = jnp.concatenate(b[0:8], axis=0)
    pq = [(p, q) for p in range(1, N_TOP) for q in range(N_TOP // (p + 1))]
    dense = [p for p in range(1, N_TOP) if N_TOP // (p + 1) > 4]
    cand += [a[p] + b8 for p in dense]
    rest = [(p, q) for p, q in pq if p not in dense]
    for k in range(0, len(rest), SLAB):
        grp = rest[k:k + SLAB]
        rows = [a[p] + b[q] for p, q in grp] + [ninf] * (SLAB - len(grp))
        cand.append(jnp.concatenate(rows, axis=0))
    c = _top_rows(cand, N_TOP)
    tau = 0.5 * (c[PEER_TOPK - 1] + c[PEER_TOPK])
    z = functools.reduce(lambda x, y: x + y, [jnp.exp(cr - c[0]) for cr in c[:PEER_TOPK]])
    th = tau - s1
    def count(conds):
        n = jnp.zeros(s1.shape, F32)
        for q, cq in enumerate(conds):
            n = jnp.where(cq, float(q + 1), n)
        return n

    cn_ref[hd, :, cols] = count([bq >= th for bq in b])
    rk_ref[hd, :, cols] = count([bq > s2 for bq in b]).astype(BF16)
    e1_ref[hd, :, cols] = jnp.exp(s1 - a[0]) * (0.5 / z)
    e2_ref[hd, :, cols] = jnp.exp(s2 - b[0]).astype(BF16)


def _peer_body(h_ref, gf_ref, wq_ref, sk_ref, u_hbm, vt_hbm, gl_ref, y_ref,
               xt_ref, qt_ref, rk_ref, cn_ref, e1_ref, e2_ref, hid0, hid1, wg0, wg1, acc_ref,
               u_buf, vt_buf, u_sem, vt_sem):
    ne = u_hbm.shape[0]
    hid_refs, wg_refs = (hid0, hid1), (wg0, wg1)

    def u_copy(e, slot):
        return pltpu.make_async_copy(u_hbm.at[e], u_buf.at[slot], u_sem.at[slot])

    def vt_copy(e, slot):
        return pltpu.make_async_copy(vt_hbm.at[e], vt_buf.at[slot], vt_sem.at[slot])

    def start_if_in_range(copy_of, e, slot):
        if isinstance(e, int):
            if e < ne:
                copy_of(e, slot).start()
        else:
            pl.when(e < ne)(lambda: copy_of(e, slot).start())

    def stage_a(rows, slot, par):
        hid_refs[par][rows, :] = jnp.dot(u_buf[slot, rows, :], xt_ref[...], preferred_element_type=F32)

    def stage_b(i, rows, par):
        w = None
        for hd in range(PEER_HEADS):
            cnt = cn_ref[hd, pl.ds(i, 1), :].astype(BF16)
            e1 = e1_ref[hd, pl.ds(i, 1), :].astype(BF16)
            wh = jnp.where(rk_ref[hd] < cnt, e2_ref[hd], jnp.zeros((), BF16)) * e1
            w = wh if w is None else w + wh
        hid = hid_refs[par][rows, :]
        act = hid * (1.0 + lax.erf(hid * math.sqrt(0.5)))
        wg_refs[par][rows, :] = w * act.astype(BF16)

    def stage_c(rows, slot, par, first):
        d = jnp.dot(vt_buf[slot, rows, :], wg_refs[par][...], preferred_element_type=F32)
        if first:
            acc_ref[rows, :] = d
        else:
            acc_ref[rows, :] += d

    def iteration(e, par):
        static = isinstance(e, int)
        do_a = (not static) or e < ne
        do_b = (not static) or 1 <= e <= ne
        do_c = (not static) or e >= 2
        sa, sc = e % N_SLOTS, (e - 2) % N_SLOTS
        if do_a:
            u_copy(e, sa).wait()
        if do_c:
            vt_copy(e - 2, sc).wait()

        def row_group(g, carry):
            nrow = ROW_GROUP * N_KEYS
            rows = pl.ds(pl.multiple_of(g * nrow, nrow), nrow)
            if do_a:
                stage_a(rows, sa, par)
            if do_b:
                for k in range(ROW_GROUP):
                    r = g * ROW_GROUP + k
                    stage_b((e - 1) * ROWS_PER_STEP + r,
                            pl.ds(pl.multiple_of(r * N_KEYS, N_KEYS), N_KEYS), 1 - par)
            if do_c:
                stage_c(rows, sc, par, static and e == 2)
            return carry

        lax.fori_loop(0, ROWS_PER_STEP // ROW_GROUP, row_group, 0)
        if do_a:
            start_if_in_range(u_copy, e + N_SLOTS, sa)
        if do_c:
            start_if_in_range(vt_copy, e - 2 + N_SLOTS, sc)

    for s in range(N_SLOTS):
        u_copy(s, s).start()
        vt_copy(s, s).start()

    hn = _rmsnorm(h_ref[...], gf_ref[...])
    xt_ref[...] = hn.T.astype(BF16)
    qt_ref[...] = jnp.dot(wq_ref[...], xt_ref[...], preferred_element_type=F32)

    def head(hd, carry):
        _peer_select(hd, qt_ref, sk_ref, rk_ref, cn_ref, e1_ref, e2_ref)
        return carry

    lax.fori_loop(0, PEER_HEADS, head, 0)

    def steady(k, carry):
        iteration(2 * k, 0)
        iteration(2 * k + 1, 1)
        return carry

    assert ne % 2 == 0
    for e in range(4):
        iteration(e, e % 2)
    lax.fori_loop(2, ne // 2, steady, 0)
    for e in range(ne, ne + 2):
        iteration(e, e % 2)

    y_ref[...] = _rmsnorm(h_ref[...] + acc_ref[...].T, gl_ref[...])


def _peer(h2, g_ffn, wq_t, sk, u, v_t, g_last):
    n = h2.shape[0]
    t = _row_block(n, 512)
    assert u.shape[0] >= N_SLOTS + 2
    return pl.pallas_call(
        _peer_body,
        grid=(n // t,),
        in_specs=[pl.BlockSpec((t, D_MODEL), lambda i: (i, 0)),
                  _const_spec((1, D_MODEL)),
                  _const_spec(wq_t.shape),
                  _const_spec(sk.shape),
                  pl.BlockSpec(memory_space=pl.ANY),
                  pl.BlockSpec(memory_space=pl.ANY),
                  _const_spec((1, D_MODEL))],
        out_specs=pl.BlockSpec((t, D_MODEL), lambda i: (i, 0)),
        out_shape=jax.ShapeDtypeStruct((n, D_MODEL), F32),
        scratch_shapes=[pltpu.VMEM((D_MODEL, t), BF16),
                        pltpu.VMEM((PEER_HEADS * 2 * PEER_HALF, t), F32),
                        pltpu.VMEM((PEER_HEADS, N_KEYS, t), BF16),
                        pltpu.VMEM((PEER_HEADS, N_KEYS, t), F32),
                        pltpu.VMEM((PEER_HEADS, N_KEYS, t), F32),
                        pltpu.VMEM((PEER_HEADS, N_KEYS, t), BF16),
                        pltpu.VMEM((E_BLOCK, t), F32),
                        pltpu.VMEM((E_BLOCK, t), F32),
                        pltpu.VMEM((E_BLOCK, t), BF16),
                        pltpu.VMEM((E_BLOCK, t), BF16),
                        pltpu.VMEM((D_MODEL, t), F32),
                        pltpu.VMEM((N_SLOTS, E_BLOCK, D_MODEL), BF16),
                        pltpu.VMEM((N_SLOTS, D_MODEL, E_BLOCK), BF16),
                        pltpu.SemaphoreType.DMA((N_SLOTS,)),
                        pltpu.SemaphoreType.DMA((N_SLOTS,))],
        compiler_params=_params("parallel"),
        name="peer",
    )(h2, g_ffn, wq_t, sk, u, v_t, g_last)


def _layer(x3, s0, kc3, vc3, w, bias_tabs):
    b, s, _ = x3.shape
    x2 = x3.reshape(b * s, D_MODEL)
    za, zb, zg = _inproj(x2, w["norm_mix"], w["w_in"])
    za3 = za.reshape(b, s, W_A)
    if kc3 is None:
        oa = _attn_prompt(za3, bias_tabs)
    else:
        oa = _attn_step(za3, kc3, vc3, *bias_tabs)
    ob, s_new = _hgrn(zb.reshape(b, s, W_B), w["lb"], w["hgrn_norm"], s0)
    h2 = _finish(x2, oa.reshape(b * s, D_ATTN), ob.reshape(b * s, D_RNN), zg,
                 w["w_a"], w["w_b"], w["w_out"])
    k_rows = za3[:, :, D_ATTN:2 * D_ATTN]
    v_rows = za3[:, :, 2 * D_ATTN:3 * D_ATTN]
    return h2, k_rows, v_rows, s_new


def kernel(x_prompt, x_sample, cache_attn_k, cache_attn_v, state_hgrn, norm_mix, w_in, rel_bias,
           lb_logits, hgrn_norm, w_branch_a, w_branch_b, w_out, norm_ffn, peer_query, peer_subkeys,
           peer_u, peer_v, norm_final):
    depth = w_in.shape[0]
    bp, sp, _ = x_prompt.shape
    bs, ss, _ = x_sample.shape
    past = cache_attn_k.shape[2]
    keep = min(ATTN_REACH, sp)
    lb_all = jnp.cumsum(jax.nn.softmax(lb_logits.astype(F32), axis=0), axis=0)
    g_last = norm_final.reshape(1, D_MODEL)

    xp, xs = x_prompt, x_sample
    outs = [[] for _ in range(6)]
    for l in range(depth):
        w = dict(
            norm_mix=norm_mix[l].reshape(1, D_MODEL),
            w_in=w_in[l].astype(BF16),
            lb=lb_all[l].reshape(1, D_RNN),
            hgrn_norm=hgrn_norm[l].reshape(1, D_RNN),
            w_a=w_branch_a[l].astype(BF16),
            w_b=w_branch_b[l].astype(BF16),
            w_out=w_out[l].astype(BF16),
        )
        g_ffn = norm_ffn[l].reshape(1, D_MODEL)
        wq_t = peer_query[l].T.astype(BF16)
        sk = peer_subkeys[l].astype(BF16)
        ne = N_EXPERTS // E_BLOCK
        u = peer_u[l].astype(BF16).reshape(ne, E_BLOCK, D_MODEL)
        v_t = peer_v[l].astype(BF16).reshape(ne, E_BLOCK, D_MODEL).transpose(0, 2, 1)
        last = l == depth - 1

        hp, kp, vp, stp = _layer(xp, None, None, None, w, _rel_bias_tables(rel_bias[l]))
        hs, ks, vs, sts = _layer(
            xs, state_hgrn[l],
            cache_attn_k[l].reshape(bs, past, D_ATTN), cache_attn_v[l].reshape(bs, past, D_ATTN),
            w, _step_bias_tables(rel_bias[l], past, ss))
        assert last, "only the final layer fuses the closing rmsnorm"
        xp = _peer(hp, g_ffn, wq_t, sk, u, v_t, g_last).reshape(bp, sp, D_MODEL)
        xs = _peer(hs, g_ffn, wq_t, sk, u, v_t, g_last).reshape(bs, ss, D_MODEL)
        outs[0].append(kp[:, sp - keep:].reshape(bp, keep, N_HEADS_A, HEAD_DIM_A))
        outs[1].append(vp[:, sp - keep:].reshape(bp, keep, N_HEADS_A, HEAD_DIM_A))
        outs[2].append(stp)
        outs[3].append(ks.reshape(bs, ss, N_HEADS_A, HEAD_DIM_A))
        outs[4].append(vs.reshape(bs, ss, N_HEADS_A, HEAD_DIM_A))
        outs[5].append(sts)
    stacked = [jnp.stack(o, axis=0) for o in outs]
    return (xp, xs, *stacked)
```

```python
import functools
import math

import jax
import jax.numpy as jnp
from jax import lax
from jax.experimental import pallas as pl
from jax.experimental.pallas import tpu as pltpu

F32 = jnp.float32
BF16 = jnp.bfloat16

D_MODEL = 1024
CHUNK = 64
N_PAST_CHUNKS = 8
ATTN_REACH = N_PAST_CHUNKS * CHUNK
N_HEADS_A = 8
HEAD_DIM_A = 64
D_ATTN = N_HEADS_A * HEAD_DIM_A
REL_CLIP = 256
ATTN_SCALE = HEAD_DIM_A ** -0.5
N_HEADS_B = 4
DK_B = 128
DV_B = 128
D_RNN = N_HEADS_B * DK_B
PEER_HEADS = 8
N_KEYS = 128
N_EXPERTS = N_KEYS * N_KEYS
PEER_HALF = 128
PEER_TOPK = 16
EPS = 1e-6

LANES = 128
Q_BLOCK = 4 * CHUNK
SUB = 8
MASKED = -1e30
VMEM_LIMIT = 56 * 1024 * 1024


def _params(*sem):
    return pltpu.CompilerParams(dimension_semantics=sem, vmem_limit_bytes=VMEM_LIMIT)


def _const_spec(shape):
    nd = len(shape)
    return pl.BlockSpec(shape, lambda *_: (0,) * nd, pipeline_mode=pl.Buffered(1))


def _row_block(n, cap):
    t = cap
    while n % t:
        t //= 2
    assert t >= LANES, (n, cap)
    return t


def _rmsnorm(x, g):
    return x * lax.rsqrt(jnp.mean(x * x, axis=-1, keepdims=True) + EPS) * g


W_A = 3 * D_ATTN
W_B = 4 * D_RNN
W_G = 2 * D_MODEL


def _inproj_body(x_ref, g_ref, w_ref, za_ref, zb_ref, zg_ref):
    xn = _rmsnorm(x_ref[...], g_ref[...]).astype(BF16)
    za_ref[...] = jnp.dot(xn, w_ref[:, 0:W_A], preferred_element_type=F32)
    zb_ref[...] = jnp.dot(xn, w_ref[:, W_A:W_A + W_B], preferred_element_type=F32)
    zg_ref[...] = jnp.dot(xn, w_ref[:, W_A + W_B:W_A + W_B + W_G], preferred_element_type=F32).astype(BF16)


def _inproj(x2, g, w):
    n = x2.shape[0]
    tm = _row_block(n, 512)
    return pl.pallas_call(
        _inproj_body,
        grid=(n // tm,),
        in_specs=[pl.BlockSpec((tm, D_MODEL), lambda i: (i, 0)),
                  _const_spec((1, D_MODEL)),
                  _const_spec((D_MODEL, W_A + W_B + W_G))],
        out_specs=[pl.BlockSpec((tm, W_A), lambda i: (i, 0)),
                   pl.BlockSpec((tm, W_B), lambda i: (i, 0)),
                   pl.BlockSpec((tm, W_G), lambda i: (i, 0))],
        out_shape=[jax.ShapeDtypeStruct((n, W_A), F32),
                   jax.ShapeDtypeStruct((n, W_B), F32),
                   jax.ShapeDtypeStruct((n, W_G), BF16)],
        compiler_params=_params("parallel"),
        name="inproj",
    )(x2, g, w)


def _attend_pair(q2, parts, bias_of):
    tq = q2.shape[0]
    first = lax.broadcasted_iota(jnp.int32, (1, LANES), 1) < HEAD_DIM_A
    qs = q2 * ATTN_SCALE
    qm = jnp.concatenate([jnp.where(first, qs, 0.0), jnp.where(first, 0.0, qs)], axis=0).astype(BF16)
    scores = []
    for pi, (k2, _) in enumerate(parts):
        s = lax.dot_general(qm, k2, (((1,), (1,)), ((), ())), preferred_element_type=F32)
        scores.append(s + jnp.concatenate([bias_of(0, pi), bias_of(1, pi)], axis=0))
    mx = functools.reduce(jnp.maximum, [jnp.max(s, axis=-1, keepdims=True) for s in scores])
    l = None
    o = None
    for s, (_, v2) in zip(scores, parts):
        p = jnp.exp(s - mx)
        ls = jnp.sum(p, axis=-1, keepdims=True)
        os_ = jnp.dot(p.astype(BF16), v2, preferred_element_type=F32)
        l = ls if l is None else l + ls
        o = os_ if o is None else o + os_
    o = o / l
    return jnp.where(first, o[:tq], o[tq:]).astype(BF16)


def _attn_prompt_body(q_ref, k_ref, v_ref, bias_ref, o_ref):
    qi = pl.program_id(1)
    win = Q_BLOCK + ATTN_REACH

    def run(kstart, nk, boff):
        for hp in range(N_HEADS_A // 2):
            ls = slice(hp * LANES, (hp + 1) * LANES)
            k2 = k_ref[0, pl.ds(kstart, nk), ls].astype(BF16)
            v2 = v_ref[0, pl.ds(kstart, nk), ls].astype(BF16)
            o_ref[0, :, ls] = _attend_pair(
                q_ref[0, :, ls], [(k2, v2)],
                lambda sub, pi: bias_ref[2 * hp + sub, :, boff:boff + nk])

    n_short = ATTN_REACH // Q_BLOCK
    for j in range(n_short):
        nk = (j + 1) * Q_BLOCK
        pl.when(qi == j)(functools.partial(run, 0, nk, win - nk))

    @pl.when(qi >= n_short)
    def _():
        run(pl.multiple_of(qi * Q_BLOCK - ATTN_REACH, Q_BLOCK), win, 0)


def _attn_prompt(za3, bias):
    b, s, _ = za3.shape
    assert s % Q_BLOCK == 0 and ATTN_REACH % Q_BLOCK == 0
    return pl.pallas_call(
        _attn_prompt_body,
        grid=(b, s // Q_BLOCK),
        in_specs=[pl.BlockSpec((1, Q_BLOCK, D_ATTN), lambda i, j: (i, j, 0)),
                  pl.BlockSpec((1, s, D_ATTN), lambda i, j: (i, 0, 1)),
                  pl.BlockSpec((1, s, D_ATTN), lambda i, j: (i, 0, 2)),
                  _const_spec(bias.shape)],
        out_specs=pl.BlockSpec((1, Q_BLOCK, D_ATTN), lambda i, j: (i, j, 0)),
        out_shape=jax.ShapeDtypeStruct((b, s, D_ATTN), BF16),
        compiler_params=_params("parallel", "arbitrary"),
        name="attn_prompt",
    )(za3, za3, za3, bias)


ATTN_STEP_STREAMS = 4


def _attn_step_body(q_ref, kn_ref, vn_ref, kc_ref, vc_ref, bc_ref, bn_ref, o_ref):
    for b in range(q_ref.shape[0]):
        for hp in range(N_HEADS_A // 2):
            ls = slice(hp * LANES, (hp + 1) * LANES)
            parts = [(kc_ref[b, :, ls].astype(BF16), vc_ref[b, :, ls].astype(BF16)),
                     (kn_ref[b, :, ls].astype(BF16), vn_ref[b, :, ls].astype(BF16))]
            o_ref[b, :, ls] = _attend_pair(
                q_ref[b, :, ls], parts,
                lambda sub, pi: (bc_ref, bn_ref)[pi][2 * hp + sub])


def _attn_step(za3, kc3, vc3, bias_c, bias_n):
    b, t, _ = za3.shape
    past = kc3.shape[1]
    nb = math.gcd(b, ATTN_STEP_STREAMS)
    return pl.pallas_call(
        _attn_step_body,
        grid=(b // nb,),
        in_specs=[pl.BlockSpec((nb, t, D_ATTN), lambda i: (i, 0, 0)),
                  pl.BlockSpec((nb, t, D_ATTN), lambda i: (i, 0, 1)),
                  pl.BlockSpec((nb, t, D_ATTN), lambda i: (i, 0, 2)),
                  pl.BlockSpec((nb, past, D_ATTN), lambda i: (i, 0, 0)),
                  pl.BlockSpec((nb, past, D_ATTN), lambda i: (i, 0, 0)),
                  _const_spec(bias_c.shape),
                  _const_spec(bias_n.shape)],
        out_specs=pl.BlockSpec((nb, t, D_ATTN), lambda i: (i, 0, 0)),
        out_shape=jax.ShapeDtypeStruct((b, t, D_ATTN), BF16),
        compiler_params=_params("parallel"),
        name="attn_step",
    )(za3, za3, za3, kc3, vc3, bias_c, bias_n)


def _toeplitz_bias(rel_bias, tq, nk, past):
    span = tq + nk - 1
    lo = past - (nk - 1) + REL_CLIP
    padl = max(0, -lo)
    padr = max(0, lo + span - 1 - 2 * REL_CLIP)
    ext = jnp.pad(rel_bias.astype(F32), ((0, 0), (padl, padr)), mode="edge")
    g = ext[:, lo + padl:lo + padl + span][:, ::-1]
    flat = jnp.tile(g, (1, tq))[:, tq - 1:tq - 1 + tq * (span - 1)]
    return flat.reshape(-1, tq, span - 1)[:, :, :nk]


def _rel_bias_tables(rel_bias):
    win = Q_BLOCK + ATTN_REACH
    t = jnp.arange(Q_BLOCK)[:, None] // CHUNK
    j = jnp.arange(win)[None, :] // CHUNK
    band = (j >= t) & (j <= t + N_PAST_CHUNKS)
    return jnp.where(band[None], _toeplitz_bias(rel_bias, Q_BLOCK, win, ATTN_REACH), MASKED)


def _step_bias_tables(rel_bias, past, t_new):
    full = _toeplitz_bias(rel_bias, t_new, past + t_new, past)
    return full[:, :, :past], full[:, :, past:]


HGRN_ROWS = CHUNK
HGRN_STREAMS = 8


def _cumsum_rows(x):
    r = x.shape[0]
    assert r & (r - 1) == 0
    hi = x.astype(BF16)
    rest = x - hi.astype(F32)
    mid = rest.astype(BF16)
    lo = (rest - mid.astype(F32)).astype(BF16)
    row = lax.broadcasted_iota(jnp.int32, (r, 3 * r), 0)
    col = lax.broadcasted_iota(jnp.int32, (r, 3 * r), 1)
    tri = jnp.where((col & (r - 1)) <= row, 1.0, 0.0).astype(BF16)
    return jnp.dot(tri, jnp.concatenate([hi, mid, lo], axis=0), preferred_element_type=F32)


def _hgrn_chunk(cum, kk, iv, qr, g, hgh, st_t):
    rows = cum.shape[0]
    qq = qr * jax.nn.sigmoid(qr)
    vb = iv.astype(BF16)
    nsub = rows // SUB

    qe = (qq * jnp.exp(cum)).astype(BF16)
    o = lax.dot_general(qe, st_t.astype(BF16), (((1,), (1,)), ((), ())), preferred_element_type=F32)

    zeros = jnp.zeros((SUB, DK_B), F32)
    refs = [cum[bi * SUB - 1:bi * SUB, :] for bi in range(1, nsub)]
    a_all = qq[SUB:] * jnp.exp(cum[SUB:] - jnp.concatenate(
        [jnp.broadcast_to(r, (SUB, DK_B)) for r in refs], axis=0))
    a_cols, b_cols = [], []
    for bi in range(1, nsub):
        lo = bi * SUB
        a_cols.append(jnp.concatenate(
            [zeros] * bi + [a_all[lo - SUB:lo]] + [zeros] * (nsub - 1 - bi), axis=0))
        b_cols.append(jnp.concatenate(
            [kk[0:lo] * jnp.exp(refs[bi - 1] - cum[0:lo])] + [zeros] * (nsub - bi), axis=0))
    att = lax.dot_general(jnp.concatenate(a_cols, axis=1).astype(BF16),
                          jnp.concatenate(b_cols, axis=1).astype(BF16),
                          (((1,), (1,)), ((), ())), preferred_element_type=F32)
    o = o + jnp.dot(att.astype(BF16), vb, preferred_element_type=F32)

    rmod = lax.broadcasted_iota(jnp.int32, (rows, DK_B), 0) % SUB

    def shift(x, d):
        return pltpu.roll(x.reshape(nsub, SUB, DK_B), d, axis=1).reshape(rows, DK_B)

    o = o + jnp.sum(qq * kk, axis=-1, keepdims=True) * iv
    ckk = cum - jnp.log(kk)
    for d in range(1, SUB):
        p = jnp.where(rmod >= d, qq * jnp.exp(cum - shift(ckk, d)), 0.0)
        o = o + jnp.sum(p, axis=-1, keepdims=True) * shift(iv, d)

    last = cum[rows - 1:rows, :]
    kd = (kk * jnp.exp(last - cum)).astype(BF16)
    st_new = st_t * jnp.exp(last) + jnp.dot(iv.T.astype(BF16), kd, preferred_element_type=F32)

    ob = o * lax.rsqrt(jnp.mean(o * o, axis=-1, keepdims=True) + EPS) * hgh
    ob = ob * (g * jax.nn.sigmoid(g))
    return ob, st_new


def _hgrn_body(*refs, has_s0):
    if has_s0:
        f_ref, i_ref, q_ref, g_ref, lb_ref, hg_ref, s0_ref, o_ref, sn_ref, st_ref = refs
    else:
        f_ref, i_ref, q_ref, g_ref, lb_ref, hg_ref, o_ref, sn_ref, st_ref = refs
    c = pl.program_id(1)
    streams = f_ref.shape[0]

    @pl.when(c == 0)
    def _():
        for b in range(streams):
            for h in range(N_HEADS_B):
                st_ref[b, h] = s0_ref[b, h].T if has_s0 else jnp.zeros((DV_B, DK_B), F32)

    lb = lb_ref[...]
    for b in range(streams):
        fg = lb + (1.0 - lb) * jax.nn.sigmoid(f_ref[b])
        cum = _cumsum_rows(jnp.log(fg))
        kk = 1.0 - fg
        for h in range(N_HEADS_B):
            ls = slice(h * LANES, (h + 1) * LANES)
            ob, st_new = _hgrn_chunk(cum[:, ls], kk[:, ls], i_ref[b, :, ls], q_ref[b, :, ls],
                                     g_ref[b, :, ls], hg_ref[:, ls], st_ref[b, h])
            o_ref[b, :, ls] = ob.astype(BF16)
            st_ref[b, h] = st_new

    @pl.when(c == pl.num_programs(1) - 1)
    def _():
        for b in range(streams):
            for h in range(N_HEADS_B):
                sn_ref[b, h] = st_ref[b, h].T


def _hgrn(zb3, lb, hg, s0):
    b, s, _ = zb3.shape
    rows = min(HGRN_ROWS, s)
    nb = math.gcd(b, HGRN_STREAMS)
    assert s % rows == 0 and rows % SUB == 0
    col = lambda k: pl.BlockSpec((nb, rows, D_RNN), lambda i, j: (i, j, k))
    in_specs = [col(0), col(1), col(2), col(3), _const_spec((1, D_RNN)), _const_spec((1, D_RNN))]
    args = [zb3, zb3, zb3, zb3, lb, hg]
    if s0 is not None:
        in_specs.append(pl.BlockSpec((nb, N_HEADS_B, DK_B, DV_B), lambda i, j: (i, 0, 0, 0)))
        args.append(s0)
    return pl.pallas_call(
        functools.partial(_hgrn_body, has_s0=s0 is not None),
        grid=(b // nb, s // rows),
        in_specs=in_specs,
        out_specs=[pl.BlockSpec((nb, rows, D_RNN), lambda i, j: (i, j, 0)),
                   pl.BlockSpec((nb, N_HEADS_B, DK_B, DV_B), lambda i, j: (i, 0, 0, 0))],
        out_shape=[jax.ShapeDtypeStruct((b, s, D_RNN), BF16),
                   jax.ShapeDtypeStruct((b, N_HEADS_B, DK_B, DV_B), F32)],
        scratch_shapes=[pltpu.VMEM((nb, N_HEADS_B, DV_B, DK_B), F32)],
        compiler_params=_params("parallel", "arbitrary"),
        name="hgrn",
    )(*args)


def _finish_body(x_ref, oa_ref, ob_ref, zg_ref, wa_ref, wb_ref, wo_ref, h_ref):
    ma = jnp.dot(oa_ref[...], wa_ref[...], preferred_element_type=F32)
    mb = jnp.dot(ob_ref[...], wb_ref[...], preferred_element_type=F32)
    merged = (jax.nn.sigmoid(zg_ref[:, 0:D_MODEL].astype(F32)) * ma
              + jax.nn.sigmoid(zg_ref[:, D_MODEL:2 * D_MODEL].astype(F32)) * mb)
    h_ref[...] = x_ref[...] + jnp.dot(merged.astype(BF16), wo_ref[...], preferred_element_type=F32)


def _finish(x2, oa2, ob2, zg, wa, wb, wo):
    n = x2.shape[0]
    tm = _row_block(n, 1024)
    row = lambda w: pl.BlockSpec((tm, w), lambda i: (i, 0))
    return pl.pallas_call(
        _finish_body,
        grid=(n // tm,),
        in_specs=[row(D_MODEL), row(D_ATTN), row(D_RNN), row(W_G),
                  _const_spec(wa.shape), _const_spec(wb.shape), _const_spec(wo.shape)],
        out_specs=row(D_MODEL),
        out_shape=jax.ShapeDtypeStruct((n, D_MODEL), F32),
        compiler_params=_params("parallel"),
        name="finish",
    )(x2, oa2, ob2, zg, wa, wb, wo)


N_TOP = PEER_TOPK + 1
E_BLOCK = D_MODEL
ROWS_PER_STEP = E_BLOCK // N_KEYS
N_SLOTS = 2
ROW_GROUP = 8


SLAB = 8


def _sort_network(n):
    pairs, p = [], 1
    while p < n:
        k = p
        while k >= 1:
            for j in range(k % p, n - k, 2 * k):
                for i in range(min(k, n - j - k)):
                    if (i + j) // (2 * p) == (i + j + k) // (2 * p):
                        pairs.append((i + j, i + j + k))
            k //= 2
        p *= 2
    return pairs


def _top_rows(slabs, n):
    v = list(slabs)
    width = 1
    while width < len(v):
        width *= 2
    for i, j in _sort_network(width):
        if j < len(v):
            v[i], v[j] = jnp.maximum(v[i], v[j]), jnp.minimum(v[i], v[j])
    rows = []
    for r in range(n):
        mx = jnp.max(v[0], axis=0, keepdims=True)
        rows.append(mx)
        if r + 1 < n:
            hit = v[0] == mx
            keep = min(len(v), n - 1 - r)
            v = [jnp.where(hit, v[k + 1] if k + 1 < len(v) else -jnp.inf, v[k]) for k in range(keep)]
    return rows


def _slabs(x):
    return [x[k * SLAB:(k + 1) * SLAB, :] for k in range(x.shape[0] // SLAB)]


def _peer_select(hd, qt_ref, sk_ref, rk_ref, cn_ref, e1_ref, e2_ref):
    t = qt_ref.shape[1]
    sc = []
    for p in range(2):
        qhp = qt_ref[pl.ds(pl.multiple_of((hd * 2 + p) * PEER_HALF, PEER_HALF), PEER_HALF), :]
        sc.append(jnp.dot(sk_ref[p, hd], qhp.astype(BF16), preferred_element_type=F32))
    for c in range(t // LANES):
        cols = slice(c * LANES, (c + 1) * LANES)
        _select_tile(hd, cols, sc[0][:, cols], sc[1][:, cols], rk_ref, cn_ref, e1_ref, e2_ref)


def _select_tile(hd, cols, s1, s2, rk_ref, cn_ref, e1_ref, e2_ref):
    t = s1.shape[1]
    a = _top_rows(_slabs(s1), N_TOP)
    b = _top_rows(_slabs(s2), N_TOP)
    ninf = jnp.full((1, t), -jnp.inf, F32)
    cand = [a[0] + jnp.concatenate(b[8 * k:8 * k + 8] + [ninf] * (8 * k + 8 - N_TOP), axis=0)
            for k in range((N_TOP + 7) // 8)]
    b8 = jnp.concatenate(b[0:8], axis=0)
    pq = [(p, q) for p in range(1, N_TOP) for q in range(N_TOP // (p + 1))]
    dense = [p for p in range(1, N_TOP) if N_TOP // (p + 1) > 4]
    cand += [a[p] + b8 for p in dense]
    rest = [(p, q) for p, q in pq if p not in dense]
    for k in range(0, len(rest), SLAB):
        grp = rest[k:k + SLAB]
        rows = [a[p] + b[q] for p, q in grp] + [ninf] * (SLAB - len(grp))
        cand.append(jnp.concatenate(rows, axis=0))
    c = _top_rows(cand, N_TOP)
    tau = 0.5 * (c[PEER_TOPK - 1] + c[PEER_TOPK])
    z = functools.reduce(lambda x, y: x + y, [jnp.exp(cr - c[0]) for cr in c[:PEER_TOPK]])
    th = tau - s1
    def count(conds):
        n = jnp.zeros(s1.shape, F32)
        for q, cq in enumerate(conds):
            n = jnp.where(cq, float(q + 1), n)
        return n

    cn_ref[hd, :, cols] = count([bq >= th for bq in b])
    rk_ref[hd, :, cols] = count([bq > s2 for bq in b]).astype(BF16)
    e1_ref[hd, :, cols] = jnp.exp(s1 - a[0]) * (0.5 / z)
    e2_ref[hd, :, cols] = jnp.exp(s2 - b[0]).astype(BF16)


def _peer_body(h_ref, gf_ref, wq_ref, sk_ref, u_hbm, vt_hbm, gl_ref, y_ref,
               xt_ref, qt_ref, rk_ref, cn_ref, e1_ref, e2_ref, hid0, hid1, wg0, wg1, acc_ref,
               u_buf, vt_buf, u_sem, vt_sem):
    ne = u_hbm.shape[0]
    hid_refs, wg_refs = (hid0, hid1), (wg0, wg1)

    def u_copy(e, slot):
        return pltpu.make_async_copy(u_hbm.at[e], u_buf.at[slot], u_sem.at[slot])

    def vt_copy(e, slot):
        return pltpu.make_async_copy(vt_hbm.at[e], vt_buf.at[slot], vt_sem.at[slot])

    def start_if_in_range(copy_of, e, slot):
        if isinstance(e, int):
            if e < ne:
                copy_of(e, slot).start()
        else:
            pl.when(e < ne)(lambda: copy_of(e, slot).start())

    def stage_a(rows, slot, par):
        hid_refs[par][rows, :] = jnp.dot(u_buf[slot, rows, :], xt_ref[...], preferred_element_type=F32)

    def stage_b(i, rows, par):
        w = None
        for hd in range(PEER_HEADS):
            cnt = cn_ref[hd, pl.ds(i, 1), :].astype(BF16)
            e1 = e1_ref[hd, pl.ds(i, 1), :].astype(BF16)
            wh = jnp.where(rk_ref[hd] < cnt, e2_ref[hd], jnp.zeros((), BF16)) * e1
            w = wh if w is None else w + wh
        hid = hid_refs[par][rows, :]
        act = hid * (1.0 + lax.erf(hid * math.sqrt(0.5)))
        wg_refs[par][rows, :] = w * act.astype(BF16)

    def stage_c(rows, slot, par, first):
        d = jnp.dot(vt_buf[slot, rows, :], wg_refs[par][...], preferred_element_type=F32)
        if first:
            acc_ref[rows, :] = d
        else:
            acc_ref[rows, :] += d

    def iteration(e, par):
        static = isinstance(e, int)
        do_a = (not static) or e < ne
        do_b = (not static) or 1 <= e <= ne
        do_c = (not static) or e >= 2
        sa, sc = e % N_SLOTS, (e - 2) % N_SLOTS
        if do_a:
            u_copy(e, sa).wait()
        if do_c:
            vt_copy(e - 2, sc).wait()

        def row_group(g, carry):
            nrow = ROW_GROUP * N_KEYS
            rows = pl.ds(pl.multiple_of(g * nrow, nrow), nrow)
            if do_a:
                stage_a(rows, sa, par)
            if do_b:
                for k in range(ROW_GROUP):
                    r = g * ROW_GROUP + k
                    stage_b((e - 1) * ROWS_PER_STEP + r,
                            pl.ds(pl.multiple_of(r * N_KEYS, N_KEYS), N_KEYS), 1 - par)
            if do_c:
                stage_c(rows, sc, par, static and e == 2)
            return carry

        lax.fori_loop(0, ROWS_PER_STEP // ROW_GROUP, row_group, 0)
        if do_a:
            start_if_in_range(u_copy, e + N_SLOTS, sa)
        if do_c:
            start_if_in_range(vt_copy, e - 2 + N_SLOTS, sc)

    for s in range(N_SLOTS):
        u_copy(s, s).start()
        vt_copy(s, s).start()

    hn = _rmsnorm(h_ref[...], gf_ref[...])
    xt_ref[...] = hn.T.astype(BF16)
    qt_ref[...] = jnp.dot(wq_ref[...], xt_ref[...], preferred_element_type=F32)

    def head(hd, carry):
        _peer_select(hd, qt_ref, sk_ref, rk_ref, cn_ref, e1_ref, e2_ref)
        return carry

    lax.fori_loop(0, PEER_HEADS, head, 0)

    def steady(k, carry):
        iteration(2 * k, 0)
        iteration(2 * k + 1, 1)
        return carry

    assert ne % 2 == 0
    for e in range(4):
        iteration(e, e % 2)
    lax.fori_loop(2, ne // 2, steady, 0)
    for e in range(ne, ne + 2):
        iteration(e, e % 2)

    y_ref[...] = _rmsnorm(h_ref[...] + acc_ref[...].T, gl_ref[...])


def _peer(h2, g_ffn, wq_t, sk, u, v_t, g_last):
    n = h2.shape[0]
    t = _row_block(n, 512)
    assert u.shape[0] >= N_SLOTS + 2
    return pl.pallas_call(
        _peer_body,
        grid=(n // t,),
        in_specs=[pl.BlockSpec((t, D_MODEL), lambda i: (i, 0)),
                  _const_spec((1, D_MODEL)),
                  _const_spec(wq_t.shape),
                  _const_spec(sk.shape),
                  pl.BlockSpec(memory_space=pl.ANY),
                  pl.BlockSpec(memory_space=pl.ANY),
                  _const_spec((1, D_MODEL))],
        out_specs=pl.BlockSpec((t, D_MODEL), lambda i: (i, 0)),
        out_shape=jax.ShapeDtypeStruct((n, D_MODEL), F32),
        scratch_shapes=[pltpu.VMEM((D_MODEL, t), BF16),
                        pltpu.VMEM((PEER_HEADS * 2 * PEER_HALF, t), F32),
                        pltpu.VMEM((PEER_HEADS, N_KEYS, t), BF16),
                        pltpu.VMEM((PEER_HEADS, N_KEYS, t), F32),
                        pltpu.VMEM((PEER_HEADS, N_KEYS, t), F32),
                        pltpu.VMEM((PEER_HEADS, N_KEYS, t), BF16),
                        pltpu.VMEM((E_BLOCK, t), F32),
                        pltpu.VMEM((E_BLOCK, t), F32),
                        pltpu.VMEM((E_BLOCK, t), BF16),
                        pltpu.VMEM((E_BLOCK, t), BF16),
                        pltpu.VMEM((D_MODEL, t), F32),
                        pltpu.VMEM((N_SLOTS, E_BLOCK, D_MODEL), BF16),
                        pltpu.VMEM((N_SLOTS, D_MODEL, E_BLOCK), BF16),
                        pltpu.SemaphoreType.DMA((N_SLOTS,)),
                        pltpu.SemaphoreType.DMA((N_SLOTS,))],
        compiler_params=_params("parallel"),
        name="peer",
    )(h2, g_ffn, wq_t, sk, u, v_t, g_last)


def _layer(x3, s0, kc3, vc3, w, bias_tabs):
    b, s, _ = x3.shape
    x2 = x3.reshape(b * s, D_MODEL)
    za, zb, zg = _inproj(x2, w["norm_mix"], w["w_in"])
    za3 = za.reshape(b, s, W_A)
    if kc3 is None:
        oa = _attn_prompt(za3, bias_tabs)
    else:
        oa = _attn_step(za3, kc3, vc3, *bias_tabs)
    ob, s_new = _hgrn(zb.reshape(b, s, W_B), w["lb"], w["hgrn_norm"], s0)
    h2 = _finish(x2, oa.reshape(b * s, D_ATTN), ob.reshape(b * s, D_RNN), zg,
                 w["w_a"], w["w_b"], w["w_out"])
    k_rows = za3[:, :, D_ATTN:2 * D_ATTN]
    v_rows = za3[:, :, 2 * D_ATTN:3 * D_ATTN]
    return h2, k_rows, v_rows, s_new


def kernel(x_prompt, x_sample, cache_attn_k, cache_attn_v, state_hgrn, norm_mix, w_in, rel_bias,
           lb_logits, hgrn_norm, w_branch_a, w_branch_b, w_out, norm_ffn, peer_query, peer_subkeys,
           peer_u, peer_v, norm_final):
    depth = w_in.shape[0]
    bp, sp, _ = x_prompt.shape
    bs, ss, _ = x_sample.shape
    past = cache_attn_k.shape[2]
    keep = min(ATTN_REACH, sp)
    lb_all = jnp.cumsum(jax.nn.softmax(lb_logits.astype(F32), axis=0), axis=0)
    g_last = norm_final.reshape(1, D_MODEL)

    xp, xs = x_prompt, x_sample
    outs = [[] for _ in range(6)]
    for l in range(depth):
        w = dict(
            norm_mix=norm_mix[l].reshape(1, D_MODEL),
            w_in=w_in[l].astype(BF16),
            lb=lb_all[l].reshape(1, D_RNN),
            hgrn_norm=hgrn_norm[l].reshape(1, D_RNN),
            w_a=w_branch_a[l].astype(BF16),
            w_b=w_branch_b[l].astype(BF16),
            w_out=w_out[l].astype(BF16),
        )
        g_ffn = norm_ffn[l].reshape(1, D_MODEL)
        wq_t = peer_query[l].T.astype(BF16)
        sk = peer_subkeys[l].astype(BF16)
        ne = N_EXPERTS // E_BLOCK
        u = peer_u[l].astype(BF16).reshape(ne, E_BLOCK, D_MODEL)
        v_t = peer_v[l].astype(BF16).reshape(ne, E_BLOCK, D_MODEL).transpose(0, 2, 1)
        last = l == depth - 1

        hp, kp, vp, stp = _layer(xp, None, None, None, w, _rel_bias_tables(rel_bias[l]))
        hs, ks, vs, sts = _layer(
            xs, state_hgrn[l],
            cache_attn_k[l].reshape(bs, past, D_ATTN), cache_attn_v[l].reshape(bs, past, D_ATTN),
            w, _step_bias_tables(rel_bias[l], past, ss))
        assert last, "only the final layer fuses the closing rmsnorm"
        xp = _peer(hp, g_ffn, wq_t, sk, u, v_t, g_last).reshape(bp, sp, D_MODEL)
        xs = _peer(hs, g_ffn, wq_t, sk, u, v_t, g_last).reshape(bs, ss, D_MODEL)
        outs[0].append(kp[:, sp - keep:].reshape(bp, keep, N_HEADS_A, HEAD_DIM_A))
        outs[1].append(vp[:, sp - keep:].reshape(bp, keep, N_HEADS_A, HEAD_DIM_A))
        outs[2].append(stp)
        outs[3].append(ks.reshape(bs, ss, N_HEADS_A, HEAD_DIM_A))
        outs[4].append(vs.reshape(bs, ss, N_HEADS_A, HEAD_DIM_A))
        outs[5].append(sts)
    stacked = [jnp.stack(o, axis=0) for o in outs]
    return (xp, xs, *stacked)
```
